```python
import jax, jax.numpy as jnp
from jax import lax
import numpy as np

D_MODEL = 1024
BATCH = 8
SEQ = 4096
DEPTH = 2
DEC_BATCH = 32
DEC_SEQ = 1
PAST_LEN = 16384
PAGE_SIZE = 128

EPS = 1e-6
MASK_VALUE = -1e30
EXP_CLAMP = 60.0
HG_HEADS = 4
HG_DK = 128
HG_DV = 128
HG_WIDTH = HG_HEADS * HG_DK
HG_VWIDTH = HG_HEADS * HG_DV
HG_CHUNK = 32
CONV_DIM = 512
CONV_W = 3
ATT_HEADS = 8
ATT_HD = 64
ATT_WIDTH = ATT_HEADS * ATT_HD
MOBA_BLOCK = 256
MOBA_TOPK = 3
MOBA_QCHUNK = 32
N_BRANCH = 3
D_FF = ((8 * D_MODEL + 3 * 256 - 1) // (3 * 256)) * 256
IN_WIDTH = 2 * HG_WIDTH + 2 * HG_VWIDTH + 3 * CONV_DIM + 3 * ATT_WIDTH + N_BRANCH * D_MODEL

kernel_name = 'hgrn2_shortconv_moba_parallel_hybrid_step'


def _rmsnorm(x, g):
    xf = x.astype(jnp.float32)
    xf = xf * lax.rsqrt(jnp.mean(xf * xf, axis=-1, keepdims=True) + EPS)
    return xf.astype(x.dtype) * g


def _split_in(p):
    sizes = (HG_WIDTH, HG_WIDTH, HG_VWIDTH, HG_VWIDTH, CONV_DIM, CONV_DIM, CONV_DIM,
             ATT_WIDTH, ATT_WIDTH, ATT_WIDTH, N_BRANCH * D_MODEL)
    offs = np.cumsum(sizes)[:-1].tolist()
    return jnp.split(p, offs, axis=-1)


def _hgrn_lower_bounds(lb_raw):
    p = jax.nn.softmax(lb_raw.astype(jnp.float32), axis=0)
    return jnp.cumsum(p, axis=0) - p[0:1]


def _hgrn_inputs(q_raw, f_raw, i_raw, lb):
    lead = q_raw.shape[:-1]
    shp = lead + (HG_HEADS, HG_DK)
    q = jax.nn.silu(q_raw).astype(jnp.float32).reshape(shp)
    a = f_raw.astype(jnp.float32)
    logf = jax.nn.log_sigmoid(a) + jnp.log1p(lb * jnp.exp(jnp.minimum(-a, EXP_CLAMP)))
    k = -jnp.expm1(logf)
    v = i_raw.astype(jnp.float32).reshape(lead + (HG_HEADS, HG_DV))
    return q, k.reshape(shp), v, logf.reshape(shp)


def _hgrn_chunked(q, k, v, g):
    B, S, H, DK = q.shape
    DV = v.shape[-1]
    C = HG_CHUNK
    NC = S // C
    r = lambda t: t.reshape(B, NC, C, H, t.shape[-1]).transpose(0, 1, 3, 2, 4)
    q, k, v, g = r(q), r(k), r(v), r(g)
    b = jnp.cumsum(g, axis=3)
    b_ref = b[:, :, :, C // 2:C // 2 + 1, :]
    a = jnp.einsum('bnhid,bnhjd->bnhij', q * jnp.exp(b - b_ref), k * jnp.exp(b_ref - b))
    causal = jnp.tril(jnp.ones((C, C), dtype=bool))
    a = jnp.where(causal, a, 0.0)
    o_intra = jnp.einsum('bnhij,bnhjv->bnhiv', a, v)
    b_last = b[:, :, :, -1:, :]
    q_dec = q * jnp.exp(b)
    k_dec = k * jnp.exp(b_last - b)
    decay = jnp.exp(b_last[:, :, :, 0, :])

    def step(s, xs):
        qd, kd, vv, dc = xs
        o = jnp.einsum('bhid,bhdv->bhiv', qd, s)
        s = dc[..., None] * s + jnp.einsum('bhjd,bhjv->bhdv', kd, vv)
        return s, o

    s0 = jnp.zeros((B, H, DK, DV), jnp.float32)
    s_fin, o_inter = lax.scan(step, s0, (jnp.moveaxis(q_dec, 1, 0), jnp.moveaxis(k_dec, 1, 0),
                                         jnp.moveaxis(v, 1, 0), jnp.moveaxis(decay, 1, 0)))
    o = o_intra + jnp.moveaxis(o_inter, 0, 1)
    return o.transpose(0, 1, 3, 2, 4).reshape(B, S, H, DV), s_fin


def _hgrn_step(q, k, v, g, s0):
    def step(s, xs):
        qt, kt, vt, gt = xs
        s = jnp.exp(gt)[..., None] * s + kt[..., None] * vt[..., None, :]
        return s, jnp.einsum('bhd,bhdv->bhv', qt, s)

    s, o = lax.scan(step, s0, tuple(jnp.moveaxis(t, 1, 0) for t in (q, k, v, g)))
    return jnp.moveaxis(o, 0, 1), s


def _hgrn_out(o, og, gain):
    o = o * lax.rsqrt(jnp.mean(o * o, axis=-1, keepdims=True) + EPS) * gain.astype(jnp.float32)
    o = o.reshape(o.shape[:-2] + (HG_VWIDTH,))
    return o.astype(og.dtype) * jax.nn.silu(og)


def _short_conv(z, buf, w):
    zz = jnp.concatenate([buf, z], axis=1)
    T = z.shape[1]
    y = w[0] * zz[:, 0:T]
    for i in range(1, CONV_W):
        y = y + w[i] * zz[:, i:i + T]
    return y, zz[:, -(CONV_W - 1):]


def _moba_attend(q, q_pos, kb, vb, means):
    B, NB, L, H, HD = kb.shape
    Q = q.shape[1]
    bq = q_pos // L
    gate = jnp.einsum('bqhd,bnhd->bhqn', q.astype(jnp.float32), means)
    past = jnp.arange(NB)[None, :] < bq[:, None]
    gate = jnp.where(past[None, None], gate, MASK_VALUE)
    n_sel = min(MOBA_TOPK, NB)
    _, idx = lax.top_k(gate, n_sel)
    sel_ok = jnp.arange(n_sel)[None, :] < bq[:, None]
    bi = jnp.arange(B)[:, None, None, None]
    hi = jnp.arange(H)[None, :, None, None]
    k_sel = kb[bi, idx, :, hi]
    v_sel = vb[bi, idx, :, hi]
    k_own = kb[:, bq]
    v_own = vb[:, bq]
    s_sel = jnp.einsum('bqhd,bhqrld->bhqrl', q, k_sel).reshape(B, H, Q, n_sel * L)
    s_own = jnp.einsum('bqhd,bqlhd->bhql', q, k_own)
    own_ok = (bq[:, None] * L + jnp.arange(L)[None, :]) <= q_pos[:, None]
    ok = jnp.concatenate([jnp.repeat(sel_ok, L, axis=1), own_ok], axis=1)
    s = jnp.concatenate([s_sel, s_own], axis=-1).astype(jnp.float32) * (ATT_HD ** -0.5)
    s = jnp.where(ok[None, None], s, MASK_VALUE)
    p = jax.nn.softmax(s, axis=-1).astype(vb.dtype)
    p_sel = p[..., :n_sel * L].reshape(B, H, Q, n_sel, L)
    p_own = p[..., n_sel * L:]
    return (jnp.einsum('bhqrl,bhqrld->bqhd', p_sel, v_sel)
            + jnp.einsum('bhql,bqlhd->bqhd', p_own, v_own))


def _moba_prompt(q, k, v):
    B, S, H, HD = q.shape
    L = MOBA_BLOCK
    T = -(-S // L) * L
    padw = ((0, 0), (0, T - S), (0, 0), (0, 0))
    kb = jnp.pad(k, padw).reshape(B, T // L, L, H, HD)
    vb = jnp.pad(v, padw).reshape(B, T // L, L, H, HD)
    means = jnp.mean(kb.astype(jnp.float32), axis=2)
    NQ = S // MOBA_QCHUNK
    qs = q.reshape(B, NQ, MOBA_QCHUNK, H, HD).transpose(1, 0, 2, 3, 4)
    ps = jnp.arange(S, dtype=jnp.int32).reshape(NQ, MOBA_QCHUNK)
    out = lax.map(lambda a: _moba_attend(a[0], a[1], kb, vb, means), (qs, ps))
    return out.transpose(1, 0, 2, 3, 4).reshape(B, S, H, HD)


def _moba_sample(q, k_new, v_new, k_past, v_past):
    B, Tn, H, HD = q.shape
    P = k_past.shape[1]
    L = MOBA_BLOCK
    T0 = P + Tn
    T = -(-T0 // L) * L
    pad = jnp.zeros((B, T - T0, H, HD), k_new.dtype)
    kb = jnp.concatenate([k_past.astype(k_new.dtype), k_new, pad], axis=1).reshape(B, T // L, L, H, HD)
    vb = jnp.concatenate([v_past.astype(v_new.dtype), v_new, pad], axis=1).reshape(B, T // L, L, H, HD)
    means = jnp.mean(kb.astype(jnp.float32), axis=2)
    pos = P + jnp.arange(Tn, dtype=jnp.int32)
    return _moba_attend(q, pos, kb, vb, means)


def _layer(x, lw, lb, hg_state, conv_buf, k_past, v_past):
    (n1, n2, w_in, b_g, hgn, cw, wa, wb, wc, wo, wu, wd) = lw
    prompt = hg_state is None
    Bx, T, _ = x.shape
    h = _rmsnorm(x, n1)
    (qa, fa, ia, oga, bg, cg, hc, qc, kc, vc, gate_raw) = _split_in(h @ w_in)
    q, k, v, g = _hgrn_inputs(qa, fa, ia, lb)
    if prompt:
        o, s_new = _hgrn_chunked(q, k, v, g)
        s_new = s_new.astype(x.dtype)
    else:
        o, s_new = _hgrn_step(q, k, v, g, hg_state.astype(jnp.float32))
        s_new = s_new.astype(hg_state.dtype)
    oa = _hgrn_out(o, oga, hgn)
    buf = jnp.zeros((Bx, CONV_W - 1, CONV_DIM), x.dtype) if prompt else conv_buf.astype(x.dtype)
    yb, buf_new = _short_conv(cg * hc, buf, cw)
    ob = bg * yb
    qh = qc.reshape(Bx, T, ATT_HEADS, ATT_HD)
    kh = kc.reshape(Bx, T, ATT_HEADS, ATT_HD)
    vh = vc.reshape(Bx, T, ATT_HEADS, ATT_HD)
    if prompt:
        oc = _moba_prompt(qh, kh, vh)
    else:
        oc = _moba_sample(qh, kh, vh, k_past, v_past)
    oc = oc.reshape(Bx, T, ATT_WIDTH)
    ga, gb, gc = jnp.split(gate_raw + b_g, N_BRANCH, axis=-1)
    mix = (jax.nn.sigmoid(ga) * (oa @ wa) + jax.nn.sigmoid(gb) * (ob @ wb)
           + jax.nn.sigmoid(gc) * (oc @ wc))
    x = x + mix @ wo
    h2 = _rmsnorm(x, n2)
    gg, uu = jnp.split(h2 @ wu, 2, axis=-1)
    x = x + (jax.nn.silu(gg) * uu) @ wd
    return x, kh, vh, s_new, buf_new


def setup_inputs(seed: int = 0) -> dict:
    key = jax.random.key(seed)
    ks = jax.random.split(key, 24)
    f32 = jnp.float32
    nrm = lambda k, shape, s: jax.random.normal(k, shape, f32) * s
    n_pages = PAST_LEN // PAGE_SIZE
    n_used = DEC_BATCH * n_pages
    n_pool = n_used + max(1, n_used // 4)
    page_table = jax.random.permutation(ks[4], n_pool)[:n_used].reshape(DEC_BATCH, n_pages).astype(jnp.int32)
    cache_shape = (n_pool, DEPTH, PAGE_SIZE, ATT_HEADS, ATT_HD)
    return {
        'x_prompt': nrm(ks[0], (BATCH, SEQ, D_MODEL), 1.0),
        'x_sample': nrm(ks[1], (DEC_BATCH, DEC_SEQ, D_MODEL), 1.0),
        'cache_k': nrm(ks[2], cache_shape, 1.0),
        'cache_v': nrm(ks[3], cache_shape, 1.0),
        'state_hgrn': nrm(ks[5], (DEC_BATCH, DEPTH, HG_HEADS, HG_DK, HG_DV), 0.3),
        'state_conv': nrm(ks[6], (DEC_BATCH, DEPTH, CONV_W - 1, CONV_DIM), 1.0),
        'page_table': page_table,
        'norm1': 1.0 + nrm(ks[7], (DEPTH, D_MODEL), 0.01),
        'norm2': 1.0 + nrm(ks[8], (DEPTH, D_MODEL), 0.01),
        'final_norm': 1.0 + nrm(ks[9], (D_MODEL,), 0.01),
        'w_in': nrm(ks[10], (DEPTH, D_MODEL, IN_WIDTH), D_MODEL ** -0.5),
        'b_gate': nrm(ks[11], (DEPTH, N_BRANCH * D_MODEL), 0.02),
        'hg_lb_raw': nrm(ks[12], (DEPTH, HG_WIDTH), 0.5),
        'hg_norm': 1.0 + nrm(ks[13], (DEPTH, HG_DV), 0.01),
        'conv_w': nrm(ks[14], (DEPTH, CONV_W, CONV_DIM), CONV_W ** -0.5),
        'w_branch_a': nrm(ks[15], (DEPTH, HG_VWIDTH, D_MODEL), HG_VWIDTH ** -0.5),
        'w_branch_b': nrm(ks[16], (DEPTH, CONV_DIM, D_MODEL), CONV_DIM ** -0.5),
        'w_branch_c': nrm(ks[17], (DEPTH, ATT_WIDTH, D_MODEL), ATT_WIDTH ** -0.5),
        'w_out': nrm(ks[18], (DEPTH, D_MODEL, D_MODEL), D_MODEL ** -0.5),
        'w_ffn_up': nrm(ks[19], (DEPTH, D_MODEL, 2 * D_FF), D_MODEL ** -0.5),
        'w_ffn_down': nrm(ks[20], (DEPTH, D_FF, D_MODEL), D_FF ** -0.5),
    }


def reference(x_prompt, x_sample, cache_k, cache_v, state_hgrn, state_conv, page_table,
              norm1, norm2, final_norm, w_in, b_gate, hg_lb_raw, hg_norm, conv_w,
              w_branch_a, w_branch_b, w_branch_c, w_out, w_ffn_up, w_ffn_down):
    lbs = _hgrn_lower_bounds(hg_lb_raw)
    n_seq = page_table.shape[0]
    xp, xs = x_prompt, x_sample
    kp_l, vp_l, sp_l, cp_l = [], [], [], []
    ks_l, vs_l, ss_l, cs_l = [], [], [], []
    for l in range(DEPTH):
        lw = (norm1[l], norm2[l], w_in[l], b_gate[l], hg_norm[l], conv_w[l],
              w_branch_a[l], w_branch_b[l], w_branch_c[l], w_out[l], w_ffn_up[l], w_ffn_down[l])
        xp, kp, vp, sp, cp = _layer(xp, lw, lbs[l], None, None, None, None)
        kp_l.append(kp); vp_l.append(vp); sp_l.append(sp); cp_l.append(cp)
        k_past = cache_k[page_table, l].reshape(n_seq, -1, ATT_HEADS, ATT_HD)
        v_past = cache_v[page_table, l].reshape(n_seq, -1, ATT_HEADS, ATT_HD)
        xs, kn, vn, sn, cn = _layer(xs, lw, lbs[l], state_hgrn[:, l], state_conv[:, l], k_past, v_past)
        ks_l.append(kn); vs_l.append(vn); ss_l.append(sn); cs_l.append(cn)
    y_prompt = _rmsnorm(xp, final_norm)
    y_sample = _rmsnorm(xs, final_norm)
    return (y_prompt, y_sample,
            jnp.stack(kp_l, axis=1), jnp.stack(vp_l, axis=1),
            jnp.stack(sp_l, axis=1), jnp.stack(cp_l, axis=1),
            jnp.stack(ks_l, axis=1), jnp.stack(vs_l, axis=1),
            jnp.stack(ss_l, axis=1), jnp.stack(cs_l, axis=1))
```

```python
import functools

import jax
import jax.numpy as jnp
from jax import lax
from jax.experimental import pallas as pl
from jax.experimental.pallas import tpu as pltpu

F32 = jnp.float32
BF16 = jnp.bfloat16

EPS = 1e-6
MASK_VALUE = -1e30
EXP_CLAMP = 60.0
HG_HEADS = 4
HG_DK = 128
HG_DV = 128
HG_WIDTH = HG_HEADS * HG_DK
HG_CHUNK = 32
CONV_DIM = 512
CONV_W = 3
ATT_HEADS = 8
ATT_HD = 64
ATT_WIDTH = ATT_HEADS * ATT_HD
MOBA_BLOCK = 256
MOBA_TOPK = 3
N_BRANCH = 3
LANES = 128
HEADS_PER_LANE_TILE = LANES // ATT_HD

OFF_QA, OFF_FA, OFF_IA, OFF_OGA = 0, 512, 1024, 1536
OFF_BG, OFF_CG, OFF_HC = 2048, 2560, 3072
OFF_QC, OFF_KC, OFF_VC = 3584, 4096, 4608
OFF_GATE = 5120

VMEM_LIMIT = 56 * 1024 * 1024
HIGHEST = lax.Precision.HIGHEST


def _params(n_axes):
    return pltpu.CompilerParams(dimension_semantics=("arbitrary",) * n_axes,
                                vmem_limit_bytes=VMEM_LIMIT)


def _dot(a, b):
    return jnp.dot(a, b, preferred_element_type=F32)


def _dot_nt(a, b, precision=None):
    return lax.dot_general(a, b, (((1,), (1,)), ((), ())),
                           preferred_element_type=F32, precision=precision)


def _dot_tn(a, b):
    return lax.dot_general(a, b, (((0,), (0,)), ((), ())), preferred_element_type=F32)


def _rms(x):
    return x * lax.rsqrt(jnp.mean(x * x, axis=-1, keepdims=True) + EPS)


def _silu(x):
    return x * jax.nn.sigmoid(x)


def _log_sigmoid(a):
    return jnp.minimum(a, 0.0) - jnp.log1p(jnp.exp(-jnp.abs(a)))


def _hgrn_gate_inputs(qa, fa, lb):
    q = _silu(qa)
    logf = _log_sigmoid(fa) + jnp.log1p(lb * jnp.exp(jnp.minimum(-fa, EXP_CLAMP)))
    t = jnp.tanh(0.5 * logf)
    k = -2.0 * t / (1.0 - t)
    return q, k, logf


def _layer_lower_bound(raw, layer):
    e = jnp.exp(raw - jnp.max(raw, axis=0, keepdims=True))
    pr = e / jnp.sum(e, axis=0, keepdims=True)
    lb = jnp.zeros_like(pr[0:1])
    for i in range(1, layer + 1):
        lb = lb + pr[i:i + 1]
    return lb


def _inproj_body(x_ref, g_ref, w_ref, o_ref, h_scr):
    @pl.when(pl.program_id(1) == 0)
    def _():
        h_scr[...] = (_rms(x_ref[...]) * g_ref[...]).astype(BF16)

    o_ref[...] = _dot(h_scr[...], w_ref[...])


def _inproj(x2d, gain, w_bf, tm, tn):
    T, D = x2d.shape
    N = w_bf.shape[1]
    return pl.pallas_call(
        _inproj_body,
        grid=(T // tm, N // tn),
        in_specs=[pl.BlockSpec((tm, D), lambda i, j: (i, 0)),
                  pl.BlockSpec((1, D), lambda i, j: (0, 0)),
                  pl.BlockSpec((D, tn), lambda i, j: (0, j))],
        out_specs=pl.BlockSpec((tm, tn), lambda i, j: (i, j)),
        out_shape=jax.ShapeDtypeStruct((T, N), F32),
        scratch_shapes=[pltpu.VMEM((tm, D), BF16)],
        compiler_params=_params(2),
        name="inproj",
    )(x2d, gain, w_bf)


def _hgrn_prompt_body(layer, nchunk, q_ref, f_ref, i_ref, og_ref, lbr_ref, gn_ref,
                      oa_ref, st_ref, q_s, k_s, lf_s, o_s, st_s):
    C = HG_CHUNK
    t = pl.program_id(2)

    @pl.when(t == 0)
    def _():
        st_s[...] = jnp.zeros_like(st_s)

    lb = _layer_lower_bound(lbr_ref[...], layer)
    q, k, logf = _hgrn_gate_inputs(q_ref[...], f_ref[...], lb)
    q_s[...] = q
    k_s[...] = k
    lf_s[...] = logf

    row = lax.broadcasted_iota(jnp.int32, (C, C), 0)
    col = lax.broadcasted_iota(jnp.int32, (C, C), 1)
    causal = row >= col
    tri = causal.astype(F32)

    def chunk(c, st):
        r0 = pl.multiple_of(c * C, C)
        rows = pl.ds(r0, C)
        qc, kc, lf, vc = q_s[rows, :], k_s[rows, :], lf_s[rows, :], i_ref[rows, :]
        b = jnp.dot(tri, lf, preferred_element_type=F32, precision=HIGHEST)
        b_ref = b[C // 2:C // 2 + 1]
        b_last = b[C - 1:C]
        a = _dot_nt((qc * jnp.exp(b - b_ref)).astype(BF16),
                    (kc * jnp.exp(b_ref - b)).astype(BF16))
        a = jnp.where(causal, a, 0.0)
        vb = vc.astype(BF16)
        o = _dot(a.astype(BF16), vb)
        o = o + _dot_nt((qc * jnp.exp(b)).astype(BF16), st.astype(BF16))
        kd = (kc * jnp.exp(b_last - b)).astype(BF16)
        st = st * jnp.exp(b_last) + _dot_tn(vb, kd)
        o_s[rows, :] = o
        return st

    st = lax.fori_loop(0, nchunk, chunk, st_s[...])
    st_s[...] = st

    o = _rms(o_s[...]) * gn_ref[...]
    oa_ref[...] = (o * _silu(og_ref[...])).astype(oa_ref.dtype)

    @pl.when(t == pl.num_programs(2) - 1)
    def _():
        st_ref[...] = st.T


def _hgrn_prompt(p3, lb_raw, gain, layer, tc):
    B, S, _ = p3.shape
    depth = lb_raw.shape[0]
    blk = lambda off: pl.BlockSpec((None, tc, HG_DK),
                                   lambda b, h, t, o=off // HG_DK: (b, t, o + h))
    return pl.pallas_call(
        functools.partial(_hgrn_prompt_body, layer, tc // HG_CHUNK),
        grid=(B, HG_HEADS, S // tc),
        in_specs=[blk(OFF_QA), blk(OFF_FA), blk(OFF_IA), blk(OFF_OGA),
                  pl.BlockSpec((depth, HG_DK), lambda b, h, t: (0, h)),
                  pl.BlockSpec((1, HG_DV), lambda b, h, t: (0, 0))],
        out_specs=[pl.BlockSpec((None, tc, HG_DV), lambda b, h, t: (b, t, h)),
                   pl.BlockSpec((None, None, HG_DK, HG_DV), lambda b, h, t: (b, h, 0, 0))],
        out_shape=[jax.ShapeDtypeStruct((B, S, HG_HEADS * HG_DV), BF16),
                   jax.ShapeDtypeStruct((B, HG_HEADS, HG_DK, HG_DV), F32)],
        scratch_shapes=[pltpu.VMEM((tc, HG_DK), F32), pltpu.VMEM((tc, HG_DK), F32),
                        pltpu.VMEM((tc, HG_DK), F32), pltpu.VMEM((tc, HG_DV), F32),
                        pltpu.VMEM((HG_DV, HG_DK), F32)],
        compiler_params=_params(3),
        name="hgrn_prompt",
    )(p3, p3, p3, p3, lb_raw, gain)


def _as_column(row):
    n = row.shape[-1]
    return jnp.broadcast_to(row, (n, n)).T


def _hgrn_sample_body(layer, q_ref, f_ref, i_ref, og_ref, lbr_ref, gn_ref, s0_ref,
                      oa_ref, st_ref):
    lb = _layer_lower_bound(lbr_ref[...], layer)
    q, k, logf = _hgrn_gate_inputs(q_ref[...], f_ref[...], lb)
    v = i_ref[...]
    s = _as_column(jnp.exp(logf)) * s0_ref[...] + _as_column(k) * v
    st_ref[...] = s
    o = jnp.sum(_as_column(q) * s, axis=0, keepdims=True)
    o = _rms(o) * gn_ref[...]
    oa_ref[...] = (o * _silu(og_ref[...])).astype(oa_ref.dtype)


def _hgrn_sample(ps3, state, lb_raw, gain, layer):
    NS = ps3.shape[0]
    depth = lb_raw.shape[0]
    blk = lambda off: pl.BlockSpec((None, 1, HG_DK), lambda b, h, o=off // HG_DK: (b, 0, o + h))
    return pl.pallas_call(
        functools.partial(_hgrn_sample_body, layer),
        grid=(NS, HG_HEADS),
        in_specs=[blk(OFF_QA), blk(OFF_FA), blk(OFF_IA), blk(OFF_OGA),
                  pl.BlockSpec((depth, HG_DK), lambda b, h: (0, h)),
                  pl.BlockSpec((1, HG_DV), lambda b, h: (0, 0)),
                  pl.BlockSpec((None, None, None, HG_DK, HG_DV),
                               lambda b, h: (b, layer, h, 0, 0))],
        out_specs=[pl.BlockSpec((None, 1, HG_DV), lambda b, h: (b, 0, h)),
                   pl.BlockSpec((None, None, HG_DK, HG_DV), lambda b, h: (b, h, 0, 0))],
        out_shape=[jax.ShapeDtypeStruct((NS, 1, HG_HEADS * HG_DV), BF16),
                   jax.ShapeDtypeStruct((NS, HG_HEADS, HG_DK, HG_DV), F32)],
        compiler_params=_params(2),
        name="hgrn_sample",
    )(ps3, ps3, ps3, ps3, lb_raw, gain, state)


CONV_PAD = 8


def _conv_prompt_body(bg_ref, cg_ref, hc_ref, w_ref, ob_ref, buf_ref, z_s):
    t = pl.program_id(1)
    tc = bg_ref.shape[0]

    @pl.when(t == 0)
    def _():
        z_s[0:CONV_PAD, :] = jnp.zeros((CONV_PAD, CONV_DIM), F32)

    z = cg_ref[...] * hc_ref[...]
    z_s[CONV_PAD:CONV_PAD + tc, :] = z
    w = w_ref[...]
    y = w[CONV_W - 1:CONV_W] * z
    for i in range(CONV_W - 1):
        shift = CONV_W - 1 - i
        y = y + w[i:i + 1] * z_s[CONV_PAD - shift:CONV_PAD - shift + tc, :]
    ob_ref[...] = (bg_ref[...] * y).astype(ob_ref.dtype)
    z_s[0:CONV_PAD, :] = z[tc - CONV_PAD:tc, :]

    @pl.when(t == pl.num_programs(1) - 1)
    def _():
        buf_ref[...] = z[tc - (CONV_W - 1):tc, :]


def _conv_prompt(p3, w, tc):
    B, S, _ = p3.shape
    blk = lambda off: pl.BlockSpec((None, tc, CONV_DIM),
                                   lambda b, t, o=off // CONV_DIM: (b, t, o))
    return pl.pallas_call(
        _conv_prompt_body,
        grid=(B, S // tc),
        in_specs=[blk(OFF_BG), blk(OFF_CG), blk(OFF_HC),
                  pl.BlockSpec((CONV_W, CONV_DIM), lambda b, t: (0, 0))],
        out_specs=[pl.BlockSpec((None, tc, CONV_DIM), lambda b, t: (b, t, 0)),
                   pl.BlockSpec((None, CONV_W - 1, CONV_DIM), lambda b, t: (b, 0, 0))],
        out_shape=[jax.ShapeDtypeStruct((B, S, CONV_DIM), BF16),
                   jax.ShapeDtypeStruct((B, CONV_W - 1, CONV_DIM), F32)],
        scratch_shapes=[pltpu.VMEM((tc + CONV_PAD, CONV_DIM), F32)],
        compiler_params=_params(2),
        name="conv_prompt",
    )(p3, p3, p3, w)


def _conv_sample_body(bg_ref, cg_ref, hc_ref, w_ref, b0_ref, b1_ref, ob_ref, z_ref):
    z = cg_ref[...] * hc_ref[...]
    w = w_ref[...]
    y = w[0:1] * b0_ref[...] + w[1:2] * b1_ref[...] + w[2:3] * z
    ob_ref[...] = (bg_ref[...] * y).astype(ob_ref.dtype)
    z_ref[...] = z


def _conv_sample(ps, w, buf0, buf1):
    NS = ps.shape[0]
    blk = lambda off: pl.BlockSpec((NS, CONV_DIM), lambda i, o=off // CONV_DIM: (0, o))
    full = pl.BlockSpec((NS, CONV_DIM), lambda i: (0, 0))
    return pl.pallas_call(
        _conv_sample_body,
        grid=(1,),
        in_specs=[blk(OFF_BG), blk(OFF_CG), blk(OFF_HC),
                  pl.BlockSpec((CONV_W, CONV_DIM), lambda i: (0, 0)), full, full],
        out_specs=[full, full],
        out_shape=[jax.ShapeDtypeStruct((NS, CONV_DIM), BF16),
                   jax.ShapeDtypeStruct((NS, CONV_DIM), F32)],
        compiler_params=_params(1),
        name="conv_sample",
    )(ps, ps, ps, w, buf0, buf1)


def _moba_prompt_body(nb, q_ref, k_ref, v_ref, oc_ref, kb_s, vb_s, mean_s):
    L = MOBA_BLOCK
    n = pl.program_id(2)

    @pl.when(n == 0)
    def _():
        kb_s[...] = k_ref[...].astype(BF16)
        vb_s[...] = v_ref[...].astype(BF16)
        for blk in range(nb):
            mean_s[blk:blk + 1, :] = jnp.mean(k_ref[blk * L:(blk + 1) * L, :], axis=0,
                                              keepdims=True)

    blk_id = lax.broadcasted_iota(jnp.int32, (L, nb), 1)
    qi = lax.broadcasted_iota(jnp.int32, (L, L), 0)
    ki = lax.broadcasted_iota(jnp.int32, (L, L), 1)
    own_rows = pl.ds(pl.multiple_of(n * L, L), L)
    outs = []
    for h in range(HEADS_PER_LANE_TILE):
        hs = slice(h * ATT_HD, (h + 1) * ATT_HD)
        qf = q_ref[:, hs]
        gate = _dot_nt(qf, mean_s[:, hs], precision=HIGHEST)
        gate = jnp.where(blk_id < n, gate, MASK_VALUE)
        cnt = jnp.zeros((L, nb), jnp.int32)
        for jp in range(nb):
            gj = gate[:, jp:jp + 1]
            beats = jnp.where(gj > gate, 1, jnp.where((gj == gate) & (blk_id > jp), 1, 0))
            cnt = cnt + beats
        bias = jnp.where((cnt < MOBA_TOPK) & (blk_id < n), 0.0, MASK_VALUE)

        qb = (qf * (ATT_HD ** -0.5)).astype(BF16)
        s = _dot_nt(qb, kb_s[own_rows, hs])
        s = jnp.where(qi >= ki, s, MASK_VALUE)
        m = jnp.max(s, axis=1, keepdims=True)
        p = jnp.exp(s - m)
        l = jnp.sum(p, axis=1, keepdims=True)
        acc = _dot(p.astype(BF16), vb_s[own_rows, hs])

        def past(j, carry):
            m, l, acc = carry
            rows = pl.ds(pl.multiple_of(j * L, L), L)
            bj = jnp.sum(jnp.where(blk_id == j, bias, 0.0), axis=1, keepdims=True)
            s = _dot_nt(qb, kb_s[rows, hs]) + bj
            m_new = jnp.maximum(m, jnp.max(s, axis=1, keepdims=True))
            alpha = jnp.exp(m - m_new)
            p = jnp.exp(s - m_new)
            l = alpha * l + jnp.sum(p, axis=1, keepdims=True)
            acc = alpha * acc + _dot(p.astype(BF16), vb_s[rows, hs])
            return m_new, l, acc

        m, l, acc = lax.fori_loop(0, n, past, (m, l, acc))
        outs.append(acc / l)
    oc_ref[...] = jnp.concatenate(outs, axis=1).astype(oc_ref.dtype)


def _moba_prompt(p3):
    B, S, _ = p3.shape
    nb = S // MOBA_BLOCK
    npair = ATT_HEADS // HEADS_PER_LANE_TILE
    return pl.pallas_call(
        functools.partial(_moba_prompt_body, nb),
        grid=(B, npair, nb),
        in_specs=[pl.BlockSpec((None, MOBA_BLOCK, LANES),
                               lambda b, g, n: (b, n, OFF_QC // LANES + g)),
                  pl.BlockSpec((None, S, LANES), lambda b, g, n: (b, 0, OFF_KC // LANES + g)),
                  pl.BlockSpec((None, S, LANES), lambda b, g, n: (b, 0, OFF_VC // LANES + g))],
        out_specs=pl.BlockSpec((None, MOBA_BLOCK, LANES), lambda b, g, n: (b, n, g)),
        out_shape=jax.ShapeDtypeStruct((B, S, ATT_WIDTH), BF16),
        scratch_shapes=[pltpu.VMEM((S, LANES), BF16), pltpu.VMEM((S, LANES), BF16),
                        pltpu.VMEM((nb, LANES), F32)],
        compiler_params=_params(3),
        name="moba_prompt",
    )(p3, p3, p3)


PAGES_PER_STEP = 16


def _moba_select_body(n_blocks, pages_per_block, pt_ref, q_ref, *refs):
    pages = refs[:PAGES_PER_STEP]
    idx_ref = refs[PAGES_PER_STEP]
    sum_s = refs[PAGES_PER_STEP + 1]
    g = pl.program_id(1)
    blocks_per_step = PAGES_PER_STEP // pages_per_block

    @pl.when(g == 0)
    def _():
        sum_s[...] = jnp.zeros_like(sum_s)

    lane = lax.broadcasted_iota(jnp.int32, sum_s.shape, 1)
    acc = sum_s[...]
    for blk in range(blocks_per_step):
        s = pages[blk * pages_per_block][...]
        for i in range(1, pages_per_block):
            s = s + pages[blk * pages_per_block + i][...]
        col = jnp.sum(s.reshape(ATT_WIDTH, s.shape[-1]), axis=1, keepdims=True)
        acc = jnp.where(lane == g * blocks_per_step + blk, col, acc)
    sum_s[...] = acc

    @pl.when(g == pl.num_programs(1) - 1)
    def _():
        means = acc * (1.0 / MOBA_BLOCK)
        hrow = lax.broadcasted_iota(jnp.int32, (ATT_HEADS, ATT_WIDTH), 0)
        hcol = lax.broadcasted_iota(jnp.int32, (ATT_HEADS, ATT_WIDTH), 1) // ATT_HD
        q_bd = jnp.where(hrow == hcol, jnp.broadcast_to(q_ref[...], (ATT_HEADS, ATT_WIDTH)), 0.0)
        gate = jnp.dot(q_bd, means, preferred_element_type=F32, precision=HIGHEST)
        ln = lax.broadcasted_iota(jnp.int32, gate.shape, 1)
        lnf = ln.astype(F32)
        gate = jnp.where(ln < n_blocks, gate, MASK_VALUE)
        out = jnp.zeros(gate.shape, F32)
        for r in range(MOBA_TOPK):
            top = jnp.max(gate, axis=1, keepdims=True)
            ix = jnp.min(jnp.where(gate == top, lnf, float(LANES)), axis=1, keepdims=True)
            out = jnp.where(ln == r, ix, out)
            gate = jnp.where(lnf == ix, -jnp.inf, gate)
        idx_ref[...] = out.astype(jnp.int32)


def _moba_select(ck_t, page_table, q3, layer):
    NS, n_pages = page_table.shape
    page = ck_t.shape[-1]
    pages_per_block = MOBA_BLOCK // page
    n_blocks = n_pages // pages_per_block
    assert n_blocks <= LANES and n_pages % PAGES_PER_STEP == 0
    page_spec = lambda i: pl.BlockSpec(
        (None, None, ATT_HEADS, ATT_HD, page),
        lambda b, g, pt, i=i: (pt[b, g * PAGES_PER_STEP + i], layer, 0, 0, 0))
    grid_spec = pltpu.PrefetchScalarGridSpec(
        num_scalar_prefetch=1,
        grid=(NS, n_pages // PAGES_PER_STEP),
        in_specs=[pl.BlockSpec((None, 1, ATT_WIDTH), lambda b, g, pt: (b, 0, 0))]
        + [page_spec(i) for i in range(PAGES_PER_STEP)],
        out_specs=pl.BlockSpec((None, ATT_HEADS, LANES), lambda b, g, pt: (b, 0, 0)),
        scratch_shapes=[pltpu.VMEM((ATT_WIDTH, LANES), F32)])
    return pl.pallas_call(
        functools.partial(_moba_select_body, n_blocks, pages_per_block),
        grid_spec=grid_spec,
        out_shape=jax.ShapeDtypeStruct((NS, ATT_HEADS, LANES), jnp.int32),
        compiler_params=_params(2),
        name="moba_select",
    )(page_table, q3, *([ck_t] * PAGES_PER_STEP))


def _moba_sample_attend_body(n_tiles, pt_ref, ix_ref, q_ref, kn_ref, vn_ref, *refs):
    k_tiles = refs[:n_tiles]
    v_tiles = refs[n_tiles:2 * n_tiles]
    o_ref = refs[2 * n_tiles]
    scale = ATT_HD ** -0.5
    q = q_ref[...]
    qb = jnp.broadcast_to(q * scale, (8, ATT_HD)).astype(BF16)
    scores = [_dot(qb, kt[...].astype(BF16))[0:1] for kt in k_tiles]
    s_own = jnp.sum(q * kn_ref[...], axis=1, keepdims=True) * scale
    m = s_own
    for s in scores:
        m = jnp.maximum(m, jnp.max(s, axis=1, keepdims=True))
    p_own = jnp.exp(s_own - m)
    l = p_own
    o = p_own * vn_ref[...]
    for s, vt in zip(scores, v_tiles):
        p = jnp.exp(s - m)
        l = l + jnp.sum(p, axis=1, keepdims=True)
        pb = jnp.broadcast_to(p, (8, p.shape[1])).astype(BF16)
        o = o + _dot_nt(pb, vt[...].astype(BF16))[0:1]
    o_ref[...] = (o / l).astype(o_ref.dtype)


def _moba_sample_attend(ck_t, cv_t, page_table, idx_flat, q4, kn4, vn4, layer):
    NS = page_table.shape[0]
    page = ck_t.shape[-1]
    pages_per_block = MOBA_BLOCK // page
    n_tiles = MOBA_TOPK * pages_per_block

    def tile_spec(i):
        r, pg = divmod(i, pages_per_block)
        return pl.BlockSpec(
            (None, None, None, ATT_HD, page),
            lambda b, h, pt, ix: (pt[b, ix[(b * ATT_HEADS + h) * MOBA_TOPK + r] * pages_per_block + pg],
                                  layer, h, 0, 0))

    vec = pl.BlockSpec((None, None, 1, ATT_HD), lambda b, h, pt, ix: (b, h, 0, 0))
    grid_spec = pltpu.PrefetchScalarGridSpec(
        num_scalar_prefetch=2,
        grid=(NS, ATT_HEADS),
        in_specs=[vec, vec, vec] + [tile_spec(i) for i in range(n_tiles)] * 2,
        out_specs=vec)
    return pl.pallas_call(
        functools.partial(_moba_sample_attend_body, n_tiles),
        grid_spec=grid_spec,
        out_shape=jax.ShapeDtypeStruct((NS, ATT_HEADS, 1, ATT_HD), BF16),
        compiler_params=_params(2),
        name="moba_sample_attend",
    )(page_table, idx_flat, q4, kn4, vn4, *([ck_t] * n_tiles), *([cv_t] * n_tiles))


def _merge_body(oa_ref, ob_ref, oc_ref, ga_ref, gb_ref, gc_ref, ba_ref, bb_ref, bc_ref,
                x_ref, wa_ref, wb_ref, wc_ref, wo_ref, out_ref):
    mix = (jax.nn.sigmoid(ga_ref[...] + ba_ref[...]) * _dot(oa_ref[...], wa_ref[...])
           + jax.nn.sigmoid(gb_ref[...] + bb_ref[...]) * _dot(ob_ref[...], wb_ref[...])
           + jax.nn.sigmoid(gc_ref[...] + bc_ref[...]) * _dot(oc_ref[...], wc_ref[...]))
    out_ref[...] = x_ref[...] + _dot(mix.astype(BF16), wo_ref[...])


def _merge(oa, ob, oc, p, b_gate, x, wa, wb, wc, wo, tm):
    T, D = x.shape
    W = oa.shape[1]
    br = pl.BlockSpec((tm, W), lambda i: (i, 0))
    gate = lambda k: pl.BlockSpec((tm, D), lambda i, k=k: (i, OFF_GATE // D + k))
    bias = lambda k: pl.BlockSpec((1, D), lambda i, k=k: (0, k))
    wbr = pl.BlockSpec((W, D), lambda i: (0, 0))
    return pl.pallas_call(
        _merge_body,
        grid=(T // tm,),
        in_specs=[br, br, br, gate(0), gate(1), gate(2), bias(0), bias(1), bias(2),
                  pl.BlockSpec((tm, D), lambda i: (i, 0)), wbr, wbr, wbr,
                  pl.BlockSpec((D, D), lambda i: (0, 0))],
        out_specs=pl.BlockSpec((tm, D), lambda i: (i, 0)),
        out_shape=jax.ShapeDtypeStruct((T, D), F32),
        compiler_params=_params(1),
        name="merge",
    )(oa, ob, oc, p, p, p, b_gate, b_gate, b_gate, x, wa, wb, wc, wo)


def _ffn_body(final, x_ref, n2_ref, wg_ref, wu_ref, wd_ref, fn_ref, out_ref, h_s, acc_s):
    f = pl.program_id(1)

    @pl.when(f == 0)
    def _():
        h_s[...] = (_rms(x_ref[...]) * n2_ref[...]).astype(BF16)
        acc_s[...] = jnp.zeros_like(acc_s)

    h = h_s[...]
    act = _silu(_dot(h, wg_ref[...])) * _dot(h, wu_ref[...])
    acc_s[...] += _dot(act.astype(BF16), wd_ref[...])

    @pl.when(f == pl.num_programs(1) - 1)
    def _():
        y = x_ref[...] + acc_s[...]
        if final:
            y = _rms(y) * fn_ref[...]
        out_ref[...] = y


def _ffn(x, n2, w_up, w_down, final_norm, final, tm, tf):
    T, D = x.shape
    d_ff = w_down.shape[0]
    nf = d_ff // tf
    return pl.pallas_call(
        functools.partial(_ffn_body, final),
        grid=(T // tm, nf),
        in_specs=[pl.BlockSpec((tm, D), lambda i, f: (i, 0)),
                  pl.BlockSpec((1, D), lambda i, f: (0, 0)),
                  pl.BlockSpec((D, tf), lambda i, f: (0, f)),
                  pl.BlockSpec((D, tf), lambda i, f: (0, nf + f)),
                  pl.BlockSpec((tf, D), lambda i, f: (f, 0)),
                  pl.BlockSpec((1, D), lambda i, f: (0, 0))],
        out_specs=pl.BlockSpec((tm, D), lambda i, f: (i, 0)),
        out_shape=jax.ShapeDtypeStruct((T, D), F32),
        scratch_shapes=[pltpu.VMEM((tm, D), BF16), pltpu.VMEM((tm, D), F32)],
        compiler_params=_params(2),
        name="ffn",
    )(x, n2, w_up, w_up, w_down, final_norm)


def _tiles(T):
    big = T >= 1024
    return dict(inproj_tm=1024 if big else T, inproj_tn=1024,
                merge_tm=512 if big else T,
                ffn_tm=512 if big else T, ffn_tf=1408)


def kernel(x_prompt, x_sample, cache_k, cache_v, state_hgrn, state_conv, page_table, norm1, norm2, final_norm, w_in, b_gate, hg_lb_raw, hg_norm, conv_w, w_branch_a, w_branch_b, w_branch_c, w_out, w_ffn_up, w_ffn_down):
    B, S, D = x_prompt.shape
    NS = x_sample.shape[0]
    depth = w_in.shape[0]
    Tp = B * S
    tp, ts = _tiles(Tp), _tiles(NS)

    w_in_b = w_in.astype(BF16)
    wa_b, wb_b, wc_b = (w.astype(BF16) for w in (w_branch_a, w_branch_b, w_branch_c))
    wo_b, wu_b, wd_b = (w.astype(BF16) for w in (w_out, w_ffn_up, w_ffn_down))
    fnorm = final_norm[None]
    ck_t = jnp.transpose(cache_k, (0, 1, 3, 4, 2))
    cv_t = jnp.transpose(cache_v, (0, 1, 3, 4, 2))

    xp = x_prompt.reshape(Tp, D)
    xs = x_sample.reshape(NS, D)
    kp_l, vp_l, sp_l, cp_l, ks_l, vs_l, ss_l, cs_l = ([] for _ in range(8))
    for l in range(depth):
        last = l == depth - 1
        n1, n2, bg = norm1[l][None], norm2[l][None], b_gate[l][None]
        gain = hg_norm[l][None]

        p = _inproj(xp, n1, w_in_b[l], tp["inproj_tm"], tp["inproj_tn"])
        p3 = p.reshape(B, S, -1)
        oa, st = _hgrn_prompt(p3, hg_lb_raw, gain, l, 512)
        ob, cbuf = _conv_prompt(p3, conv_w[l], 512)
        oc = _moba_prompt(p3)
        x1 = _merge(oa.reshape(Tp, -1), ob.reshape(Tp, -1), oc.reshape(Tp, -1), p, bg, xp,
                    wa_b[l], wb_b[l], wc_b[l], wo_b[l], tp["merge_tm"])
        xp = _ffn(x1, n2, wu_b[l], wd_b[l], fnorm, last, tp["ffn_tm"], tp["ffn_tf"])
        kp_l.append(p3[:, :, OFF_KC:OFF_KC + ATT_WIDTH].reshape(B, S, ATT_HEADS, ATT_HD))
        vp_l.append(p3[:, :, OFF_VC:OFF_VC + ATT_WIDTH].reshape(B, S, ATT_HEADS, ATT_HD))
        sp_l.append(st)
        cp_l.append(cbuf)

        ps = _inproj(xs, n1, w_in_b[l], ts["inproj_tm"], ts["inproj_tn"])
        ps3 = ps.reshape(NS, 1, -1)
        oa_s, st_s = _hgrn_sample(ps3, state_hgrn, hg_lb_raw, gain, l)
        ob_s, z_s = _conv_sample(ps, conv_w[l], state_conv[:, l, 0], state_conv[:, l, 1])
        q_s = ps[:, OFF_QC:OFF_QC + ATT_WIDTH]
        k_s = ps[:, OFF_KC:OFF_KC + ATT_WIDTH]
        v_s = ps[:, OFF_VC:OFF_VC + ATT_WIDTH]
        sel = _moba_select(ck_t, page_table, q_s.reshape(NS, 1, ATT_WIDTH), l)
        idx = sel[:, :, :MOBA_TOPK].reshape(-1)
        to4 = lambda a: a.reshape(NS, ATT_HEADS, 1, ATT_HD)
        oc_s = _moba_sample_attend(ck_t, cv_t, page_table, idx, to4(q_s), to4(k_s), to4(v_s), l)
        x1s = _merge(oa_s.reshape(NS, -1), ob_s, oc_s.reshape(NS, -1), ps, bg, xs,
                     wa_b[l], wb_b[l], wc_b[l], wo_b[l], ts["merge_tm"])
        xs = _ffn(x1s, n2, wu_b[l], wd_b[l], fnorm, last, ts["ffn_tm"], ts["ffn_tf"])
        ks_l.append(k_s.reshape(NS, 1, ATT_HEADS, ATT_HD))
        vs_l.append(v_s.reshape(NS, 1, ATT_HEADS, ATT_HD))
        ss_l.append(st_s)
        cs_l.append(jnp.stack([state_conv[:, l, 1], z_s], axis=1))

    return (xp.reshape(B, S, D), xs.reshape(NS, 1, D),
            jnp.stack(kp_l, axis=1), jnp.stack(vp_l, axis=1),
            jnp.stack(sp_l, axis=1), jnp.stack(cp_l, axis=1),
            jnp.stack(ks_l, axis=1), jnp.stack(vs_l, axis=1),
            jnp.stack(ss_l, axis=1), jnp.stack(cs_l, axis=1))
```

```python
import functools

import jax
import jax.numpy as jnp
from jax import lax
from jax.experimental import pallas as pl
from jax.experimental.pallas import tpu as pltpu

F32 = jnp.float32
BF16 = jnp.bfloat16

EPS = 1e-6
MASK_VALUE = -1e30
EXP_CLAMP = 60.0
HG_HEADS = 4
HG_DK = 128
HG_DV = 128
HG_WIDTH = HG_HEADS * HG_DK
HG_CHUNK = 32
CONV_DIM = 512
CONV_W = 3
ATT_HEADS = 8
ATT_HD = 64
ATT_WIDTH = ATT_HEADS * ATT_HD
MOBA_BLOCK = 256
MOBA_TOPK = 3
N_BRANCH = 3
LANES = 128
HEADS_PER_LANE_TILE = LANES // ATT_HD

OFF_QA, OFF_FA, OFF_IA, OFF_OGA = 0, 512, 1024, 1536
OFF_BG, OFF_CG, OFF_HC = 2048, 2560, 3072
OFF_QC, OFF_KC, OFF_VC = 3584, 4096, 4608
OFF_GATE = 5120

VMEM_LIMIT = 56 * 1024 * 1024
HIGHEST = lax.Precision.HIGHEST


def _params(n_axes):
    return pltpu.CompilerParams(dimension_semantics=("arbitrary",) * n_axes,
                                vmem_limit_bytes=VMEM_LIMIT)


def _dot(a, b):
    return jnp.dot(a, b, preferred_element_type=F32)


def _dot_nt(a, b, precision=None):
    return lax.dot_general(a, b, (((1,), (1,)), ((), ())),
                           preferred_element_type=F32, precision=precision)


def _dot_tn(a, b):
    return lax.dot_general(a, b, (((0,), (0,)), ((), ())), preferred_element_type=F32)


def _rms(x):
    return x * lax.rsqrt(jnp.mean(x * x, axis=-1, keepdims=True) + EPS)


def _silu(x):
    return x * jax.nn.sigmoid(x)


def _log_sigmoid(a):
    return jnp.minimum(a, 0.0) - jnp.log1p(jnp.exp(-jnp.abs(a)))


def _hgrn_gate_inputs(qa, fa, lb):
    q = _silu(qa)
    logf = _log_sigmoid(fa) + jnp.log1p(lb * jnp.exp(jnp.minimum(-fa, EXP_CLAMP)))
    t = jnp.tanh(0.5 * logf)
    k = -2.0 * t / (1.0 - t)
    return q, k, logf


def _layer_lower_bound(raw, layer):
    e = jnp.exp(raw - jnp.max(raw, axis=0, keepdims=True))
    pr = e / jnp.sum(e, axis=0, keepdims=True)
    lb = jnp.zeros_like(pr[0:1])
    for i in range(1, layer + 1):
        lb = lb + pr[i:i + 1]
    return lb


def _inproj_body(x_ref, g_ref, w_ref, o_ref, h_scr):
    @pl.when(pl.program_id(1) == 0)
    def _():
        h_scr[...] = (_rms(x_ref[...]) * g_ref[...]).astype(BF16)

    o_ref[...] = _dot(h_scr[...], w_ref[...])


def _inproj(x2d, gain, w_bf, tm, tn):
    T, D = x2d.shape
    N = w_bf.shape[1]
    return pl.pallas_call(
        _inproj_body,
        grid=(T // tm, N // tn),
        in_specs=[pl.BlockSpec((tm, D), lambda i, j: (i, 0)),
                  pl.BlockSpec((1, D), lambda i, j: (0, 0)),
                  pl.BlockSpec((D, tn), lambda i, j: (0, j))],
        out_specs=pl.BlockSpec((tm, tn), lambda i, j: (i, j)),
        out_shape=jax.ShapeDtypeStruct((T, N), F32),
        scratch_shapes=[pltpu.VMEM((tm, D), BF16)],
        compiler_params=_params(2),
        name="inproj",
    )(x2d, gain, w_bf)


def _chunk_cumsum(x, chunk):
    rin = lax.broadcasted_iota(jnp.int32, x.shape, 0) % chunk
    shift = 1
    while shift < chunk:
        x = x + jnp.where(rin >= shift, pltpu.roll(x, shift, 0), 0.0)
        shift *= 2
    return x


def _hgrn_prompt_body(layer, nchunk, q_ref, f_ref, i_ref, og_ref, lbr_ref, gn_ref,
                      oa_ref, st_ref, st_s):
    C = HG_CHUNK
    t = pl.program_id(2)
    tc = q_ref.shape[0]

    @pl.when(t == 0)
    def _():
        st_s[...] = jnp.zeros_like(st_s)

    lb = _layer_lower_bound(lbr_ref[...], layer)
    q, k, logf = _hgrn_gate_inputs(q_ref[...], f_ref[...], lb)
    b = _chunk_cumsum(logf, C)

    def per_chunk_row(r):
        rows = b.reshape(nchunk, C, HG_DK)[:, r:r + 1, :]
        return jnp.broadcast_to(rows, (nchunk, C, HG_DK)).reshape(tc, HG_DK)

    b_ref = per_chunk_row(C // 2)
    b_last = per_chunk_row(C - 1)
    qx = (q * jnp.exp(b - b_ref)).astype(BF16)
    kx = (k * jnp.exp(b_ref - b)).astype(BF16)
    qd = (q * jnp.exp(b)).astype(BF16)
    kd = (k * jnp.exp(b_last - b)).astype(BF16)
    decay = jnp.exp(b_last)
    vb = i_ref[...].astype(BF16)

    row = lax.broadcasted_iota(jnp.int32, (C, C), 0)
    col = lax.broadcasted_iota(jnp.int32, (C, C), 1)
    causal = row >= col

    st = st_s[...]
    outs = []
    for c in range(nchunk):
        rows = slice(c * C, (c + 1) * C)
        a = jnp.where(causal, _dot_nt(qx[rows], kx[rows]), 0.0)
        o = _dot(a.astype(BF16), vb[rows]) + _dot_nt(qd[rows], st.astype(BF16))
        st = st * decay[c * C:c * C + 1] + _dot_tn(vb[rows], kd[rows])
        outs.append(o)
    st_s[...] = st

    o = _rms(jnp.concatenate(outs, axis=0)) * gn_ref[...]
    oa_ref[...] = (o * _silu(og_ref[...])).astype(oa_ref.dtype)

    @pl.when(t == pl.num_programs(2) - 1)
    def _():
        st_ref[...] = st.T


def _hgrn_prompt(p3, lb_raw, gain, layer, tc):
    B, S, _ = p3.shape
    depth = lb_raw.shape[0]
    blk = lambda off: pl.BlockSpec((None, tc, HG_DK),
                                   lambda b, h, t, o=off // HG_DK: (b, t, o + h))
    return pl.pallas_call(
        functools.partial(_hgrn_prompt_body, layer, tc // HG_CHUNK),
        grid=(B, HG_HEADS, S // tc),
        in_specs=[blk(OFF_QA), blk(OFF_FA), blk(OFF_IA), blk(OFF_OGA),
                  pl.BlockSpec((depth, HG_DK), lambda b, h, t: (0, h)),
                  pl.BlockSpec((1, HG_DV), lambda b, h, t: (0, 0))],
        out_specs=[pl.BlockSpec((None, tc, HG_DV), lambda b, h, t: (b, t, h)),
                   pl.BlockSpec((None, None, HG_DK, HG_DV), lambda b, h, t: (b, h, 0, 0))],
        out_shape=[jax.ShapeDtypeStruct((B, S, HG_HEADS * HG_DV), BF16),
                   jax.ShapeDtypeStruct((B, HG_HEADS, HG_DK, HG_DV), F32)],
        scratch_shapes=[pltpu.VMEM((HG_DV, HG_DK), F32)],
        compiler_params=_params(3),
        name="hgrn_prompt",
    )(p3, p3, p3, p3, lb_raw, gain)


def _as_column(row):
    n = row.shape[-1]
    return jnp.broadcast_to(row, (n, n)).T


def _hgrn_sample_body(layer, q_ref, f_ref, i_ref, og_ref, lbr_ref, gn_ref, s0_ref,
                      oa_ref, st_ref):
    lb = _layer_lower_bound(lbr_ref[...], layer)
    q, k, logf = _hgrn_gate_inputs(q_ref[...], f_ref[...], lb)
    v = i_ref[...]
    s = _as_column(jnp.exp(logf)) * s0_ref[...] + _as_column(k) * v
    st_ref[...] = s
    o = jnp.sum(_as_column(q) * s, axis=0, keepdims=True)
    o = _rms(o) * gn_ref[...]
    oa_ref[...] = (o * _silu(og_ref[...])).astype(oa_ref.dtype)


def _hgrn_sample(ps3, state, lb_raw, gain, layer):
    NS = ps3.shape[0]
    depth = lb_raw.shape[0]
    blk = lambda off: pl.BlockSpec((None, 1, HG_DK), lambda b, h, o=off // HG_DK: (b, 0, o + h))
    return pl.pallas_call(
        functools.partial(_hgrn_sample_body, layer),
        grid=(NS, HG_HEADS),
        in_specs=[blk(OFF_QA), blk(OFF_FA), blk(OFF_IA), blk(OFF_OGA),
                  pl.BlockSpec((depth, HG_DK), lambda b, h: (0, h)),
                  pl.BlockSpec((1, HG_DV), lambda b, h: (0, 0)),
                  pl.BlockSpec((None, None, None, HG_DK, HG_DV),
                               lambda b, h: (b, layer, h, 0, 0))],
        out_specs=[pl.BlockSpec((None, 1, HG_DV), lambda b, h: (b, 0, h)),
                   pl.BlockSpec((None, None, HG_DK, HG_DV), lambda b, h: (b, h, 0, 0))],
        out_shape=[jax.ShapeDtypeStruct((NS, 1, HG_HEADS * HG_DV), BF16),
                   jax.ShapeDtypeStruct((NS, HG_HEADS, HG_DK, HG_DV), F32)],
        compiler_params=_params(2),
        name="hgrn_sample",
    )(ps3, ps3, ps3, ps3, lb_raw, gain, state)


CONV_PAD = 8


def _conv_prompt_body(bg_ref, cg_ref, hc_ref, w_ref, ob_ref, buf_ref, z_s):
    t = pl.program_id(1)
    tc = bg_ref.shape[0]

    @pl.when(t == 0)
    def _():
        z_s[0:CONV_PAD, :] = jnp.zeros((CONV_PAD, CONV_DIM), F32)

    z = cg_ref[...] * hc_ref[...]
    z_s[CONV_PAD:CONV_PAD + tc, :] = z
    w = w_ref[...]
    y = w[CONV_W - 1:CONV_W] * z
    for i in range(CONV_W - 1):
        shift = CONV_W - 1 - i
        y = y + w[i:i + 1] * z_s[CONV_PAD - shift:CONV_PAD - shift + tc, :]
    ob_ref[...] = (bg_ref[...] * y).astype(ob_ref.dtype)
    z_s[0:CONV_PAD, :] = z[tc - CONV_PAD:tc, :]

    @pl.when(t == pl.num_programs(1) - 1)
    def _():
        buf_ref[...] = z[tc - (CONV_W - 1):tc, :]


def _conv_prompt(p3, w, tc):
    B, S, _ = p3.shape
    blk = lambda off: pl.BlockSpec((None, tc, CONV_DIM),
                                   lambda b, t, o=off // CONV_DIM: (b, t, o))
    return pl.pallas_call(
        _conv_prompt_body,
        grid=(B, S // tc),
        in_specs=[blk(OFF_BG), blk(OFF_CG), blk(OFF_HC),
                  pl.BlockSpec((CONV_W, CONV_DIM), lambda b, t: (0, 0))],
        out_specs=[pl.BlockSpec((None, tc, CONV_DIM), lambda b, t: (b, t, 0)),
                   pl.BlockSpec((None, CONV_W - 1, CONV_DIM), lambda b, t: (b, 0, 0))],
        out_shape=[jax.ShapeDtypeStruct((B, S, CONV_DIM), BF16),
                   jax.ShapeDtypeStruct((B, CONV_W - 1, CONV_DIM), F32)],
        scratch_shapes=[pltpu.VMEM((tc + CONV_PAD, CONV_DIM), F32)],
        compiler_params=_params(2),
        name="conv_prompt",
    )(p3, p3, p3, w)


def _conv_sample_body(bg_ref, cg_ref, hc_ref, w_ref, b0_ref, b1_ref, ob_ref, z_ref):
    z = cg_ref[...] * hc_ref[...]
    w = w_ref[...]
    y = w[0:1] * b0_ref[...] + w[1:2] * b1_ref[...] + w[2:3] * z
    ob_ref[...] = (bg_ref[...] * y).astype(ob_ref.dtype)
    z_ref[...] = z


def _conv_sample(ps, w, buf0, buf1):
    NS = ps.shape[0]
    blk = lambda off: pl.BlockSpec((NS, CONV_DIM), lambda i, o=off // CONV_DIM: (0, o))
    full = pl.BlockSpec((NS, CONV_DIM), lambda i: (0, 0))
    return pl.pallas_call(
        _conv_sample_body,
        grid=(1,),
        in_specs=[blk(OFF_BG), blk(OFF_CG), blk(OFF_HC),
                  pl.BlockSpec((CONV_W, CONV_DIM), lambda i: (0, 0)), full, full],
        out_specs=[full, full],
        out_shape=[jax.ShapeDtypeStruct((NS, CONV_DIM), BF16),
                   jax.ShapeDtypeStruct((NS, CONV_DIM), F32)],
        compiler_params=_params(1),
        name="conv_sample",
    )(ps, ps, ps, w, buf0, buf1)


def _moba_prompt_body(nb, q_ref, k_ref, v_ref, oc_ref, kt_ref, vt_ref,
                      kb_s, vt_s, mean_s, msplit_s, bias_s, s_buf, p_buf):
    L = MOBA_BLOCK
    n = pl.program_id(2)
    nh = HEADS_PER_LANE_TILE

    @pl.when(n == 0)
    def _():
        for blk in range(nb):
            rows = slice(blk * L, (blk + 1) * L)
            kblk = k_ref[rows, :]
            kb_s[blk] = kblk.astype(BF16)
            mean_s[blk:blk + 1, :] = jnp.mean(kblk, axis=0, keepdims=True)
            kt_ref[:, rows] = kblk.T
            vt = v_ref[rows, :].T
            vt_ref[:, rows] = vt
            vt_s[blk] = vt.astype(BF16)
        mean = mean_s[...]
        mean_head = lax.broadcasted_iota(jnp.int32, mean.shape, 1) // ATT_HD
        for h in range(nh):
            mh = jnp.where(mean_head == h, mean, 0.0)
            hi = mh.astype(BF16)
            msplit_s[h * nb:(h + 1) * nb, :] = hi
            msplit_s[(nh + h) * nb:(nh + h + 1) * nb, :] = (mh - hi.astype(F32)).astype(BF16)

    blk_id = lax.broadcasted_iota(jnp.int32, (nb, L), 0)
    lane_head = lax.broadcasted_iota(jnp.int32, (L, LANES), 1) // ATT_HD
    qf = q_ref[...]
    q_hi = qf.astype(BF16)
    q_lo = (qf - q_hi.astype(F32)).astype(BF16)
    g_hi = _dot_nt(msplit_s[...], q_hi)
    gates = g_hi[:nh * nb] + g_hi[nh * nb:] + _dot_nt(msplit_s[:nh * nb, :], q_lo)
    qbs = []
    for h in range(nh):
        qh = jnp.where(lane_head == h, qf, 0.0)
        gate = jnp.where(blk_id < n, gates[h * nb:(h + 1) * nb], MASK_VALUE)
        cnt = jnp.zeros((nb, L), jnp.int32)
        for jp in range(nb):
            gj = gate[jp:jp + 1, :]
            beats = jnp.where(gj > gate, 1, jnp.where((gj == gate) & (blk_id > jp), 1, 0))
            cnt = cnt + beats
        bias_s[h] = jnp.where((cnt < MOBA_TOPK) & (blk_id < n), 0.0, MASK_VALUE)
        qbs.append((qh * (ATT_HD ** -0.5)).astype(BF16))

    ki = lax.broadcasted_iota(jnp.int32, (L, L), 0)
    qi = lax.broadcasted_iota(jnp.int32, (L, L), 1)
    npc = L // LANES
    pieces = [(h, c) for h in range(nh) for c in range(npc)]

    def pv_update(blk, slot, alpha, acc):
        vt = vt_s[blk]
        out = []
        for h in range(nh):
            pv = _dot(vt[h * ATT_HD:(h + 1) * ATT_HD], p_buf[slot, h])
            out += [alpha[h * npc + c] * acc[h * npc + c] + pv[:, c * LANES:(c + 1) * LANES]
                    for c in range(npc)]
        return out

    def softmax_update(slot, bias_rows, m, l):
        m_out, l_out, alpha = [], [], []
        for i, (h, c) in enumerate(pieces):
            cols = slice(c * LANES, (c + 1) * LANES)
            s = s_buf[slot, h, :, cols]
            b = bias_rows[h][:, cols]
            m_new = jnp.maximum(m[i], jnp.max(s, axis=0, keepdims=True) + b)
            a = jnp.exp(m[i] - m_new)
            p = jnp.exp(s - (m_new - b))
            p_buf[slot, h, :, cols] = p.astype(BF16)
            m_out.append(m_new)
            l_out.append(a * l[i] + jnp.sum(p, axis=0, keepdims=True))
            alpha.append(a)
        return m_out, l_out, alpha

    def scores_into(slot, blk):
        kblk = kb_s[blk]
        for h in range(nh):
            s_buf[slot, h] = _dot_nt(kblk, qbs[h])

    k_own = kb_s[n]
    for h in range(nh):
        s_buf[1, h] = jnp.where(ki <= qi, _dot_nt(k_own, qbs[h]), MASK_VALUE)
    scores_into(0, 0)
    no_bias = [jnp.zeros((1, L), F32)] * nh
    m, l, alpha = softmax_update(1, no_bias, [jnp.full((1, LANES), -jnp.inf, F32)] * len(pieces),
                                 [jnp.zeros((1, LANES), F32)] * len(pieces))
    acc = [jnp.zeros((ATT_HD, LANES), F32)] * len(pieces)

    def stage(prev_blk, blk, next_blk, cur, oth, m, l, alpha, acc):
        acc = pv_update(prev_blk, oth, alpha, acc)
        scores_into(oth, next_blk)
        m, l, alpha = softmax_update(cur, [bias_s[h, pl.ds(blk, 1), :] for h in range(nh)], m, l)
        return m, l, alpha, acc

    def double_step(t, carry):
        m, l, alpha, acc = (list(c) for c in carry)
        j = 2 * t
        m, l, alpha, acc = stage(jnp.where(t == 0, n, j - 1), j, j + 1, 0, 1, m, l, alpha, acc)
        m, l, alpha, acc = stage(j, j + 1, jnp.minimum(j + 2, nb - 1), 1, 0, m, l, alpha, acc)
        return tuple(m), tuple(l), tuple(alpha), tuple(acc)

    n_double = (n + 1) // 2
    m, l, alpha, acc = lax.fori_loop(0, n_double, double_step,
                                     (tuple(m), tuple(l), tuple(alpha), tuple(acc)))
    acc = pv_update(jnp.where(n == 0, n, 2 * n_double - 1), 1, alpha, acc)
    out_t = jnp.concatenate(
        [jnp.concatenate([acc[h * npc + c] / l[h * npc + c] for c in range(npc)], axis=1)
         for h in range(nh)], axis=0)
    oc_ref[...] = out_t.T.astype(oc_ref.dtype)


def _moba_prompt(p3):
    B, S, _ = p3.shape
    nb = S // MOBA_BLOCK
    npair = ATT_HEADS // HEADS_PER_LANE_TILE
    return pl.pallas_call(
        functools.partial(_moba_prompt_body, nb),
        grid=(B, npair, nb),
        in_specs=[pl.BlockSpec((None, MOBA_BLOCK, LANES),
                               lambda b, g, n: (b, n, OFF_QC // LANES + g)),
                  pl.BlockSpec((None, S, LANES), lambda b, g, n: (b, 0, OFF_KC // LANES + g)),
                  pl.BlockSpec((None, S, LANES), lambda b, g, n: (b, 0, OFF_VC // LANES + g))],
        out_specs=[pl.BlockSpec((None, MOBA_BLOCK, LANES), lambda b, g, n: (b, n, g)),
                   pl.BlockSpec((None, LANES, S), lambda b, g, n: (b, g, 0)),
                   pl.BlockSpec((None, LANES, S), lambda b, g, n: (b, g, 0))],
        out_shape=[jax.ShapeDtypeStruct((B, S, ATT_WIDTH), BF16),
                   jax.ShapeDtypeStruct((B, ATT_WIDTH, S), F32),
                   jax.ShapeDtypeStruct((B, ATT_WIDTH, S), F32)],
        scratch_shapes=[pltpu.VMEM((nb, MOBA_BLOCK, LANES), BF16),
                        pltpu.VMEM((nb, LANES, MOBA_BLOCK), BF16),
                        pltpu.VMEM((nb, LANES), F32),
                        pltpu.VMEM((2 * HEADS_PER_LANE_TILE * nb, LANES), BF16),
                        pltpu.VMEM((HEADS_PER_LANE_TILE, nb, MOBA_BLOCK), F32),
                        pltpu.VMEM((2, HEADS_PER_LANE_TILE, MOBA_BLOCK, MOBA_BLOCK), F32),
                        pltpu.VMEM((2, HEADS_PER_LANE_TILE, MOBA_BLOCK, MOBA_BLOCK), BF16)],
        compiler_params=_params(3),
        name="moba_prompt",
    )(p3, p3, p3)


PAGES_PER_STEP = 16


def _moba_select_body(n_blocks, pages_per_block, pt_ref, q_ref, *refs):
    pages = refs[:PAGES_PER_STEP]
    idx_ref = refs[PAGES_PER_STEP]
    sum_s = refs[PAGES_PER_STEP + 1]
    g = pl.program_id(1)
    blocks_per_step = PAGES_PER_STEP // pages_per_block

    @pl.when(g == 0)
    def _():
        sum_s[...] = jnp.zeros_like(sum_s)

    lane = lax.broadcasted_iota(jnp.int32, sum_s.shape, 1)
    acc = sum_s[...]
    for blk in range(blocks_per_step):
        s = pages[blk * pages_per_block][...]
        for i in range(1, pages_per_block):
            s = s + pages[blk * pages_per_block + i][...]
        col = jnp.sum(s.reshape(ATT_WIDTH, s.shape[-1]), axis=1, keepdims=True)
        acc = jnp.where(lane == g * blocks_per_step + blk, col, acc)
    sum_s[...] = acc

    @pl.when(g == pl.num_programs(1) - 1)
    def _():
        means = acc * (1.0 / MOBA_BLOCK)
        hrow = lax.broadcasted_iota(jnp.int32, (ATT_HEADS, ATT_WIDTH), 0)
        hcol = lax.broadcasted_iota(jnp.int32, (ATT_HEADS, ATT_WIDTH), 1) // ATT_HD
        q_bd = jnp.where(hrow == hcol, jnp.broadcast_to(q_ref[...], (ATT_HEADS, ATT_WIDTH)), 0.0)
        gate = jnp.dot(q_bd, means, preferred_element_type=F32, precision=HIGHEST)
        ln = lax.broadcasted_iota(jnp.int32, gate.shape, 1)
        lnf = ln.astype(F32)
        gate = jnp.where(ln < n_blocks, gate, MASK_VALUE)
        out = jnp.zeros(gate.shape, F32)
        for r in range(MOBA_TOPK):
            top = jnp.max(gate, axis=1, keepdims=True)
            ix = jnp.min(jnp.where(gate == top, lnf, float(LANES)), axis=1, keepdims=True)
            out = jnp.where(ln == r, ix, out)
            gate = jnp.where(lnf == ix, -jnp.inf, gate)
        idx_ref[...] = out.astype(jnp.int32)


def _moba_select(ck_t, page_table, q3, layer):
    NS, n_pages = page_table.shape
    page = ck_t.shape[-1]
    pages_per_block = MOBA_BLOCK // page
    n_blocks = n_pages // pages_per_block
    assert n_blocks <= LANES and n_pages % PAGES_PER_STEP == 0
    page_spec = lambda i: pl.BlockSpec(
        (None, None, ATT_HEADS, ATT_HD, page),
        lambda b, g, pt, i=i: (pt[b, g * PAGES_PER_STEP + i], layer, 0, 0, 0))
    grid_spec = pltpu.PrefetchScalarGridSpec(
        num_scalar_prefetch=1,
        grid=(NS, n_pages // PAGES_PER_STEP),
        in_specs=[pl.BlockSpec((None, 1, ATT_WIDTH), lambda b, g, pt: (b, 0, 0))]
        + [page_spec(i) for i in range(PAGES_PER_STEP)],
        out_specs=pl.BlockSpec((None, ATT_HEADS, LANES), lambda b, g, pt: (b, 0, 0)),
        scratch_shapes=[pltpu.VMEM((ATT_WIDTH, LANES), F32)])
    return pl.pallas_call(
        functools.partial(_moba_select_body, n_blocks, pages_per_block),
        grid_spec=grid_spec,
        out_shape=jax.ShapeDtypeStruct((NS, ATT_HEADS, LANES), jnp.int32),
        compiler_params=_params(2),
        name="moba_select",
    )(page_table, q3, *([ck_t] * PAGES_PER_STEP))


def _moba_sample_attend_body(n_tiles, pt_ref, ix_ref, q_ref, kn_ref, vn_ref, *refs):
    k_tiles = refs[:n_tiles]
    v_tiles = refs[n_tiles:2 * n_tiles]
    o_ref = refs[2 * n_tiles]
    scale = ATT_HD ** -0.5
    q = q_ref[...]
    qb = jnp.broadcast_to(q * scale, (8, ATT_HD)).astype(BF16)
    scores = [_dot(qb, kt[...].astype(BF16))[0:1] for kt in k_tiles]
    s_own = jnp.sum(q * kn_ref[...], axis=1, keepdims=True) * scale
    m = s_own
    for s in scores:
        m = jnp.maximum(m, jnp.max(s, axis=1, keepdims=True))
    p_own = jnp.exp(s_own - m)
    l = p_own
    o = p_own * vn_ref[...]
    for s, vt in zip(scores, v_tiles):
        p = jnp.exp(s - m)
        l = l + jnp.sum(p, axis=1, keepdims=True)
        pb = jnp.broadcast_to(p, (8, p.shape[1])).astype(BF16)
        o = o + _dot_nt(pb, vt[...].astype(BF16))[0:1]
    o_ref[...] = (o / l).astype(o_ref.dtype)


def _moba_sample_attend(ck_t, cv_t, page_table, idx_flat, q4, kn4, vn4, layer):
    NS = page_table.shape[0]
    page = ck_t.shape[-1]
    pages_per_block = MOBA_BLOCK // page
    n_tiles = MOBA_TOPK * pages_per_block

    def tile_spec(i):
        r, pg = divmod(i, pages_per_block)
        return pl.BlockSpec(
            (None, None, None, ATT_HD, page),
            lambda b, h, pt, ix: (pt[b, ix[(b * ATT_HEADS + h) * MOBA_TOPK + r] * pages_per_block + pg],
                                  layer, h, 0, 0))

    vec = pl.BlockSpec((None, None, 1, ATT_HD), lambda b, h, pt, ix: (b, h, 0, 0))
    grid_spec = pltpu.PrefetchScalarGridSpec(
        num_scalar_prefetch=2,
        grid=(NS, ATT_HEADS),
        in_specs=[vec, vec, vec] + [tile_spec(i) for i in range(n_tiles)] * 2,
        out_specs=vec)
    return pl.pallas_call(
        functools.partial(_moba_sample_attend_body, n_tiles),
        grid_spec=grid_spec,
        out_shape=jax.ShapeDtypeStruct((NS, ATT_HEADS, 1, ATT_HD), BF16),
        compiler_params=_params(2),
        name="moba_sample_attend",
    )(page_table, idx_flat, q4, kn4, vn4, *([ck_t] * n_tiles), *([cv_t] * n_tiles))


def _merge_body(oa_ref, ob_ref, oc_ref, ga_ref, gb_ref, gc_ref, ba_ref, bb_ref, bc_ref,
                x_ref, wa_ref, wb_ref, wc_ref, wo_ref, out_ref):
    mix = (jax.nn.sigmoid(ga_ref[...] + ba_ref[...]) * _dot(oa_ref[...], wa_ref[...])
           + jax.nn.sigmoid(gb_ref[...] + bb_ref[...]) * _dot(ob_ref[...], wb_ref[...])
           + jax.nn.sigmoid(gc_ref[...] + bc_ref[...]) * _dot(oc_ref[...], wc_ref[...]))
    out_ref[...] = x_ref[...] + _dot(mix.astype(BF16), wo_ref[...])


def _merge(oa, ob, oc, p, b_gate, x, wa, wb, wc, wo, tm):
    T, D = x.shape
    W = oa.shape[1]
    br = pl.BlockSpec((tm, W), lambda i: (i, 0))
    gate = lambda k: pl.BlockSpec((tm, D), lambda i, k=k: (i, OFF_GATE // D + k))
    bias = lambda k: pl.BlockSpec((1, D), lambda i, k=k: (0, k))
    wbr = pl.BlockSpec((W, D), lambda i: (0, 0))
    return pl.pallas_call(
        _merge_body,
        grid=(T // tm,),
        in_specs=[br, br, br, gate(0), gate(1), gate(2), bias(0), bias(1), bias(2),
                  pl.BlockSpec((tm, D), lambda i: (i, 0)), wbr, wbr, wbr,
                  pl.BlockSpec((D, D), lambda i: (0, 0))],
        out_specs=pl.BlockSpec((tm, D), lambda i: (i, 0)),
        out_shape=jax.ShapeDtypeStruct((T, D), F32),
        compiler_params=_params(1),
        name="merge",
    )(oa, ob, oc, p, p, p, b_gate, b_gate, b_gate, x, wa, wb, wc, wo)


def _ffn_body(final, x_ref, n2_ref, wg_ref, wu_ref, wd_ref, fn_ref, out_ref, h_s, acc_s):
    f = pl.program_id(1)

    @pl.when(f == 0)
    def _():
        h_s[...] = (_rms(x_ref[...]) * n2_ref[...]).astype(BF16)
        acc_s[...] = jnp.zeros_like(acc_s)

    h = h_s[...]
    act = _silu(_dot(h, wg_ref[...])) * _dot(h, wu_ref[...])
    acc_s[...] += _dot(act.astype(BF16), wd_ref[...])

    @pl.when(f == pl.num_programs(1) - 1)
    def _():
        y = x_ref[...] + acc_s[...]
        if final:
            y = _rms(y) * fn_ref[...]
        out_ref[...] = y


def _ffn(x, n2, w_up, w_down, final_norm, final, tm, tf):
    T, D = x.shape
    d_ff = w_down.shape[0]
    nf = d_ff // tf
    return pl.pallas_call(
        functools.partial(_ffn_body, final),
        grid=(T // tm, nf),
        in_specs=[pl.BlockSpec((tm, D), lambda i, f: (i, 0)),
                  pl.BlockSpec((1, D), lambda i, f: (0, 0)),
                  pl.BlockSpec((D, tf), lambda i, f: (0, f)),
                  pl.BlockSpec((D, tf), lambda i, f: (0, nf + f)),
                  pl.BlockSpec((tf, D), lambda i, f: (f, 0)),
                  pl.BlockSpec((1, D), lambda i, f: (0, 0))],
        out_specs=pl.BlockSpec((tm, D), lambda i, f: (i, 0)),
        out_shape=jax.ShapeDtypeStruct((T, D), F32),
        scratch_shapes=[pltpu.VMEM((tm, D), BF16), pltpu.VMEM((tm, D), F32)],
        compiler_params=_params(2),
        name="ffn",
    )(x, n2, w_up, w_up, w_down, final_norm)


def _tiles(T):
    big = T >= 1024
    return dict(inproj_tm=1024 if big else T, inproj_tn=1024,
                merge_tm=512 if big else T,
                ffn_tm=512 if big else T, ffn_tf=1408)


def kernel(x_prompt, x_sample, cache_k, cache_v, state_hgrn, state_conv, page_table, norm1, norm2, final_norm, w_in, b_gate, hg_lb_raw, hg_norm, conv_w, w_branch_a, w_branch_b, w_branch_c, w_out, w_ffn_up, w_ffn_down):
    B, S, D = x_prompt.shape
    NS = x_sample.shape[0]
    depth = w_in.shape[0]
    Tp = B * S
    tp, ts = _tiles(Tp), _tiles(NS)

    w_in_b = w_in.astype(BF16)
    wa_b, wb_b, wc_b = (w.astype(BF16) for w in (w_branch_a, w_branch_b, w_branch_c))
    wo_b, wu_b, wd_b = (w.astype(BF16) for w in (w_out, w_ffn_up, w_ffn_down))
    fnorm = final_norm[None]
    ck_t = jnp.transpose(cache_k, (0, 1, 3, 4, 2))
    cv_t = jnp.transpose(cache_v, (0, 1, 3, 4, 2))

    xp = x_prompt.reshape(Tp, D)
    xs = x_sample.reshape(NS, D)
    kp_l, vp_l, sp_l, cp_l, ks_l, vs_l, ss_l, cs_l = ([] for _ in range(8))
    for l in range(depth):
        last = l == depth - 1
        n1, n2, bg = norm1[l][None], norm2[l][None], b_gate[l][None]
        gain = hg_norm[l][None]

        p = _inproj(xp, n1, w_in_b[l], tp["inproj_tm"], tp["inproj_tn"])
        p3 = p.reshape(B, S, -1)
        oa, st = _hgrn_prompt(p3, hg_lb_raw, gain, l, 512)
        ob, cbuf = _conv_prompt(p3, conv_w[l], 512)
        oc, kt, vt = _moba_prompt(p3)
        x1 = _merge(oa.reshape(Tp, -1), ob.reshape(Tp, -1), oc.reshape(Tp, -1), p, bg, xp,
                    wa_b[l], wb_b[l], wc_b[l], wo_b[l], tp["merge_tm"])
        xp = _ffn(x1, n2, wu_b[l], wd_b[l], fnorm, last, tp["ffn_tm"], tp["ffn_tf"])
        kp_l.append(kt)
        vp_l.append(vt)
        sp_l.append(st)
        cp_l.append(cbuf)

        ps = _inproj(xs, n1, w_in_b[l], ts["inproj_tm"], ts["inproj_tn"])
        ps3 = ps.reshape(NS, 1, -1)
        oa_s, st_s = _hgrn_sample(ps3, state_hgrn, hg_lb_raw, gain, l)
        ob_s, z_s = _conv_sample(ps, conv_w[l], state_conv[:, l, 0], state_conv[:, l, 1])
        q_s = ps[:, OFF_QC:OFF_QC + ATT_WIDTH]
        k_s = ps[:, OFF_KC:OFF_KC + ATT_WIDTH]
        v_s = ps[:, OFF_VC:OFF_VC + ATT_WIDTH]
        sel = _moba_select(ck_t, page_table, q_s.reshape(NS, 1, ATT_WIDTH), l)
        idx = sel[:, :, :MOBA_TOPK].reshape(-1)
        to4 = lambda a: a.reshape(NS, ATT_HEADS, 1, ATT_HD)
        oc_s = _moba_sample_attend(ck_t, cv_t, page_table, idx, to4(q_s), to4(k_s), to4(v_s), l)
        x1s = _merge(oa_s.reshape(NS, -1), ob_s, oc_s.reshape(NS, -1), ps, bg, xs,
                     wa_b[l], wb_b[l], wc_b[l], wo_b[l], ts["merge_tm"])
        xs = _ffn(x1s, n2, wu_b[l], wd_b[l], fnorm, last, ts["ffn_tm"], ts["ffn_tf"])
        ks_l.append(k_s.reshape(NS, 1, ATT_HEADS, ATT_HD))
        vs_l.append(v_s.reshape(NS, 1, ATT_HEADS, ATT_HD))
        ss_l.append(st_s)
        cs_l.append(jnp.stack([state_conv[:, l, 1], z_s], axis=1))

    from_t = lambda a: jnp.transpose(
        jnp.stack(a, axis=1).reshape(B, depth, ATT_HEADS, ATT_HD, S), (0, 1, 4, 2, 3))
    return (xp.reshape(B, S, D), xs.reshape(NS, 1, D),
            from_t(kp_l), from_t(vp_l),
            jnp.stack(sp_l, axis=1), jnp.stack(cp_l, axis=1),
            jnp.stack(ks_l, axis=1), jnp.stack(vs_l, axis=1),
            jnp.stack(ss_l, axis=1), jnp.stack(cs_l, axis=1))
```

```python
import functools

import jax
import jax.numpy as jnp
from jax import lax
from jax.experimental import pallas as pl
from jax.experimental.pallas import tpu as pltpu

F32 = jnp.float32
BF16 = jnp.bfloat16

EPS = 1e-6
MASK_VALUE = -1e30
EXP_CLAMP = 60.0
HG_HEADS = 4
HG_DK = 128
HG_DV = 128
HG_WIDTH = HG_HEADS * HG_DK
HG_CHUNK = 32
CONV_DIM = 512
CONV_W = 3
ATT_HEADS = 8
ATT_HD = 64
ATT_WIDTH = ATT_HEADS * ATT_HD
MOBA_BLOCK = 256
MOBA_TOPK = 3
N_BRANCH = 3
LANES = 128
HEADS_PER_LANE_TILE = LANES // ATT_HD

OFF_QA, OFF_FA, OFF_IA, OFF_OGA = 0, 512, 1024, 1536
OFF_BG, OFF_CG, OFF_HC = 2048, 2560, 3072
OFF_QC, OFF_KC, OFF_VC = 3584, 4096, 4608
OFF_GATE = 5120

VMEM_LIMIT = 56 * 1024 * 1024
HIGHEST = lax.Precision.HIGHEST


def _params(n_axes):
    return pltpu.CompilerParams(dimension_semantics=("arbitrary",) * n_axes,
                                vmem_limit_bytes=VMEM_LIMIT)


def _dot(a, b):
    return jnp.dot(a, b, preferred_element_type=F32)


def _dot_nt(a, b, precision=None):
    return lax.dot_general(a, b, (((1,), (1,)), ((), ())),
                           preferred_element_type=F32, precision=precision)


def _dot_tn(a, b):
    return lax.dot_general(a, b, (((0,), (0,)), ((), ())), preferred_element_type=F32)


def _rms(x):
    return x * lax.rsqrt(jnp.mean(x * x, axis=-1, keepdims=True) + EPS)


def _silu(x):
    h = 0.5 * x
    return h + h * jnp.tanh(h)


def _log_sigmoid(a):
    return jnp.minimum(a, 0.0) - jnp.log1p(jnp.exp(-jnp.abs(a)))


def _hgrn_gate_inputs(qa, fa, lb):
    q = _silu(qa)
    logf = _log_sigmoid(fa) + jnp.log1p(lb * jnp.exp(jnp.minimum(-fa, EXP_CLAMP)))
    t = jnp.tanh(0.5 * logf)
    k = -2.0 * t / (1.0 - t)
    return q, k, logf


def _layer_lower_bound(raw, layer):
    e = jnp.exp(raw - jnp.max(raw, axis=0, keepdims=True))
    pr = e / jnp.sum(e, axis=0, keepdims=True)
    lb = jnp.zeros_like(pr[0:1])
    for i in range(1, layer + 1):
        lb = lb + pr[i:i + 1]
    return lb


def _inproj_body(x_ref, g_ref, w_ref, o_ref, h_scr):
    @pl.when(pl.program_id(1) == 0)
    def _():
        h_scr[...] = (_rms(x_ref[...]) * g_ref[...]).astype(BF16)

    o_ref[...] = _dot(h_scr[...], w_ref[...])


def _inproj(x2d, gain, w_bf, tm, tn):
    T, D = x2d.shape
    N = w_bf.shape[1]
    return pl.pallas_call(
        _inproj_body,
        grid=(T // tm, N // tn),
        in_specs=[pl.BlockSpec((tm, D), lambda i, j: (i, 0)),
                  pl.BlockSpec((1, D), lambda i, j: (0, 0)),
                  pl.BlockSpec((D, tn), lambda i, j: (0, j))],
        out_specs=pl.BlockSpec((tm, tn), lambda i, j: (i, j)),
        out_shape=jax.ShapeDtypeStruct((T, N), F32),
        scratch_shapes=[pltpu.VMEM((tm, D), BF16)],
        compiler_params=_params(2),
        name="inproj",
    )(x2d, gain, w_bf)


def _chunk_cumsum(x, chunk):
    rin = lax.broadcasted_iota(jnp.int32, x.shape, 0) % chunk
    shift = 1
    while shift < chunk:
        x = x + jnp.where(rin >= shift, pltpu.roll(x, shift, 0), 0.0)
        shift *= 2
    return x


def _hgrn_prompt_body(layer, nchunk, q_ref, f_ref, i_ref, og_ref, lbr_ref, gn_ref,
                      oa_ref, st_ref, st_s):
    C = HG_CHUNK
    t = pl.program_id(2)
    tc = q_ref.shape[0]

    @pl.when(t == 0)
    def _():
        st_s[...] = jnp.zeros_like(st_s)

    lb = _layer_lower_bound(lbr_ref[...], layer)
    q, k, logf = _hgrn_gate_inputs(q_ref[...], f_ref[...], lb)
    b = _chunk_cumsum(logf, C)

    def per_chunk_row(r):
        rows = b.reshape(nchunk, C, HG_DK)[:, r:r + 1, :]
        return jnp.broadcast_to(rows, (nchunk, C, HG_DK)).reshape(tc, HG_DK)

    b_ref = per_chunk_row(C // 2)
    b_last = per_chunk_row(C - 1)
    qx = (q * jnp.exp(b - b_ref)).astype(BF16)
    kx = (k * jnp.exp(b_ref - b)).astype(BF16)
    qd = (q * jnp.exp(b)).astype(BF16)
    kd = (k * jnp.exp(b_last - b)).astype(BF16)
    decay = jnp.exp(b_last)
    vb = i_ref[...].astype(BF16)

    row = lax.broadcasted_iota(jnp.int32, (C, C), 0)
    col = lax.broadcasted_iota(jnp.int32, (C, C), 1)
    causal = row >= col

    st = st_s[...]
    outs = []
    for c in range(nchunk):
        rows = slice(c * C, (c + 1) * C)
        a = jnp.where(causal, _dot_nt(qx[rows], kx[rows]), 0.0)
        o = _dot(a.astype(BF16), vb[rows]) + _dot_nt(qd[rows], st.astype(BF16))
        st = st * decay[c * C:c * C + 1] + _dot_tn(vb[rows], kd[rows])
        outs.append(o)
    st_s[...] = st

    o = _rms(jnp.concatenate(outs, axis=0)) * gn_ref[...]
    oa_ref[...] = (o * _silu(og_ref[...])).astype(oa_ref.dtype)

    @pl.when(t == pl.num_programs(2) - 1)
    def _():
        st_ref[...] = st.T


def _hgrn_prompt(p3, lb_raw, gain, layer, tc):
    B, S, _ = p3.shape
    depth = lb_raw.shape[0]
    blk = lambda off: pl.BlockSpec((None, tc, HG_DK),
                                   lambda b, h, t, o=off // HG_DK: (b, t, o + h))
    return pl.pallas_call(
        functools.partial(_hgrn_prompt_body, layer, tc // HG_CHUNK),
        grid=(B, HG_HEADS, S // tc),
        in_specs=[blk(OFF_QA), blk(OFF_FA), blk(OFF_IA), blk(OFF_OGA),
                  pl.BlockSpec((depth, HG_DK), lambda b, h, t: (0, h)),
                  pl.BlockSpec((1, HG_DV), lambda b, h, t: (0, 0))],
        out_specs=[pl.BlockSpec((None, tc, HG_DV), lambda b, h, t: (b, t, h)),
                   pl.BlockSpec((None, None, HG_DK, HG_DV), lambda b, h, t: (b, h, 0, 0))],
        out_shape=[jax.ShapeDtypeStruct((B, S, HG_HEADS * HG_DV), BF16),
                   jax.ShapeDtypeStruct((B, HG_HEADS, HG_DK, HG_DV), F32)],
        scratch_shapes=[pltpu.VMEM((HG_DV, HG_DK), F32)],
        compiler_params=_params(3),
        name="hgrn_prompt",
    )(p3, p3, p3, p3, lb_raw, gain)


def _as_column(row):
    n = row.shape[-1]
    return jnp.broadcast_to(row, (n, n)).T


def _hgrn_sample_body(layer, q_ref, f_ref, i_ref, og_ref, lbr_ref, gn_ref, s0_ref,
                      oa_ref, st_ref):
    lb = _layer_lower_bound(lbr_ref[...], layer)
    q, k, logf = _hgrn_gate_inputs(q_ref[...], f_ref[...], lb)
    v = i_ref[...]
    s = _as_column(jnp.exp(logf)) * s0_ref[...] + _as_column(k) * v
    st_ref[...] = s
    o = jnp.sum(_as_column(q) * s, axis=0, keepdims=True)
    o = _rms(o) * gn_ref[...]
    oa_ref[...] = (o * _silu(og_ref[...])).astype(oa_ref.dtype)


def _hgrn_sample(ps3, state, lb_raw, gain, layer):
    NS = ps3.shape[0]
    depth = lb_raw.shape[0]
    blk = lambda off: pl.BlockSpec((None, 1, HG_DK), lambda b, h, o=off // HG_DK: (b, 0, o + h))
    return pl.pallas_call(
        functools.partial(_hgrn_sample_body, layer),
        grid=(NS, HG_HEADS),
        in_specs=[blk(OFF_QA), blk(OFF_FA), blk(OFF_IA), blk(OFF_OGA),
                  pl.BlockSpec((depth, HG_DK), lambda b, h: (0, h)),
                  pl.BlockSpec((1, HG_DV), lambda b, h: (0, 0)),
                  pl.BlockSpec((None, None, None, HG_DK, HG_DV),
                               lambda b, h: (b, layer, h, 0, 0))],
        out_specs=[pl.BlockSpec((None, 1, HG_DV), lambda b, h: (b, 0, h)),
                   pl.BlockSpec((None, None, HG_DK, HG_DV), lambda b, h: (b, h, 0, 0))],
        out_shape=[jax.ShapeDtypeStruct((NS, 1, HG_HEADS * HG_DV), BF16),
                   jax.ShapeDtypeStruct((NS, HG_HEADS, HG_DK, HG_DV), F32)],
        compiler_params=_params(2),
        name="hgrn_sample",
    )(ps3, ps3, ps3, ps3, lb_raw, gain, state)


CONV_PAD = 8


def _conv_prompt_body(bg_ref, cg_ref, hc_ref, w_ref, ob_ref, buf_ref, z_s):
    t = pl.program_id(1)
    tc = bg_ref.shape[0]

    @pl.when(t == 0)
    def _():
        z_s[0:CONV_PAD, :] = jnp.zeros((CONV_PAD, CONV_DIM), F32)

    z = cg_ref[...] * hc_ref[...]
    z_s[CONV_PAD:CONV_PAD + tc, :] = z
    w = w_ref[...]
    y = w[CONV_W - 1:CONV_W] * z
    for i in range(CONV_W - 1):
        shift = CONV_W - 1 - i
        y = y + w[i:i + 1] * z_s[CONV_PAD - shift:CONV_PAD - shift + tc, :]
    ob_ref[...] = (bg_ref[...] * y).astype(ob_ref.dtype)
    z_s[0:CONV_PAD, :] = z[tc - CONV_PAD:tc, :]

    @pl.when(t == pl.num_programs(1) - 1)
    def _():
        buf_ref[...] = z[tc - (CONV_W - 1):tc, :]


def _conv_prompt(p3, w, tc):
    B, S, _ = p3.shape
    blk = lambda off: pl.BlockSpec((None, tc, CONV_DIM),
                                   lambda b, t, o=off // CONV_DIM: (b, t, o))
    return pl.pallas_call(
        _conv_prompt_body,
        grid=(B, S // tc),
        in_specs=[blk(OFF_BG), blk(OFF_CG), blk(OFF_HC),
                  pl.BlockSpec((CONV_W, CONV_DIM), lambda b, t: (0, 0))],
        out_specs=[pl.BlockSpec((None, tc, CONV_DIM), lambda b, t: (b, t, 0)),
                   pl.BlockSpec((None, CONV_W - 1, CONV_DIM), lambda b, t: (b, 0, 0))],
        out_shape=[jax.ShapeDtypeStruct((B, S, CONV_DIM), BF16),
                   jax.ShapeDtypeStruct((B, CONV_W - 1, CONV_DIM), F32)],
        scratch_shapes=[pltpu.VMEM((tc + CONV_PAD, CONV_DIM), F32)],
        compiler_params=_params(2),
        name="conv_prompt",
    )(p3, p3, p3, w)


def _conv_sample_body(bg_ref, cg_ref, hc_ref, w_ref, b0_ref, b1_ref, ob_ref, z_ref):
    z = cg_ref[...] * hc_ref[...]
    w = w_ref[...]
    y = w[0:1] * b0_ref[...] + w[1:2] * b1_ref[...] + w[2:3] * z
    ob_ref[...] = (bg_ref[...] * y).astype(ob_ref.dtype)
    z_ref[...] = z


def _conv_sample(ps, w, buf0, buf1):
    NS = ps.shape[0]
    blk = lambda off: pl.BlockSpec((NS, CONV_DIM), lambda i, o=off // CONV_DIM: (0, o))
    full = pl.BlockSpec((NS, CONV_DIM), lambda i: (0, 0))
    return pl.pallas_call(
        _conv_sample_body,
        grid=(1,),
        in_specs=[blk(OFF_BG), blk(OFF_CG), blk(OFF_HC),
                  pl.BlockSpec((CONV_W, CONV_DIM), lambda i: (0, 0)), full, full],
        out_specs=[full, full],
        out_shape=[jax.ShapeDtypeStruct((NS, CONV_DIM), BF16),
                   jax.ShapeDtypeStruct((NS, CONV_DIM), F32)],
        compiler_params=_params(1),
        name="conv_sample",
    )(ps, ps, ps, w, buf0, buf1)


def _moba_prompt_body(nb, n_carried, q_ref, k_ref, v_ref, *refs):
    (oc_ref, kt_ref, vt_ref, kb_s, vt_s, mean_s, msplit_s, bias_s, s_buf, p_buf) = refs[n_carried:]
    L = MOBA_BLOCK
    nh = HEADS_PER_LANE_TILE

    for blk in range(nb):
        rows = slice(blk * L, (blk + 1) * L)
        kblk = k_ref[rows, :]
        kb_s[blk] = kblk.astype(BF16)
        mean_s[blk:blk + 1, :] = jnp.mean(kblk, axis=0, keepdims=True)
        kt_ref[:, rows] = kblk.T
        vt = v_ref[rows, :].T
        vt_ref[:, rows] = vt
        vt_s[blk] = vt.astype(BF16)
    mean = mean_s[...]
    mean_head = lax.broadcasted_iota(jnp.int32, mean.shape, 1) // ATT_HD
    for h in range(nh):
        mh = jnp.where(mean_head == h, mean, 0.0)
        hi = mh.astype(BF16)
        msplit_s[h * nb:(h + 1) * nb, :] = hi
        msplit_s[(nh + h) * nb:(nh + h + 1) * nb, :] = (mh - hi.astype(F32)).astype(BF16)

    def query_block(n, carry):
        _moba_query_block(n, nb, q_ref, oc_ref, kb_s, vt_s, msplit_s, bias_s, s_buf, p_buf)
        return carry

    lax.fori_loop(0, nb, query_block, 0)


def _moba_query_block(n, nb, q_ref, oc_ref, kb_s, vt_s, msplit_s, bias_s, s_buf, p_buf):
    L = MOBA_BLOCK
    nh = HEADS_PER_LANE_TILE
    q_rows = pl.ds(pl.multiple_of(n * L, L), L)
    blk_id = lax.broadcasted_iota(jnp.int32, (nb, L), 0)
    lane_head = lax.broadcasted_iota(jnp.int32, (L, LANES), 1) // ATT_HD
    qf = q_ref[q_rows, :]
    q_hi = qf.astype(BF16)
    q_lo = (qf - q_hi.astype(F32)).astype(BF16)
    g_hi = _dot_nt(msplit_s[...], q_hi)
    gates = g_hi[:nh * nb] + g_hi[nh * nb:] + _dot_nt(msplit_s[:nh * nb, :], q_lo)
    qbs = []
    for h in range(nh):
        qh = jnp.where(lane_head == h, qf, 0.0)
        gate = jnp.where(blk_id < n, gates[h * nb:(h + 1) * nb], MASK_VALUE)
        cnt = jnp.zeros((nb, L), jnp.int32)
        for jp in range(nb):
            gj = gate[jp:jp + 1, :]
            beats = jnp.where(gj > gate, 1, jnp.where((gj == gate) & (blk_id > jp), 1, 0))
            cnt = cnt + beats
        bias_s[h] = jnp.where((cnt < MOBA_TOPK) & (blk_id < n), 0.0, MASK_VALUE)
        qbs.append((qh * (ATT_HD ** -0.5)).astype(BF16))

    ki = lax.broadcasted_iota(jnp.int32, (L, L), 0)
    qi = lax.broadcasted_iota(jnp.int32, (L, L), 1)
    npc = L // LANES
    pieces = [(h, c) for h in range(nh) for c in range(npc)]

    def pv_update(blk, slot, alpha, acc):
        vt = vt_s[blk]
        out = []
        for h in range(nh):
            pv = _dot(vt[h * ATT_HD:(h + 1) * ATT_HD], p_buf[slot, h])
            out += [alpha[h * npc + c] * acc[h * npc + c] + pv[:, c * LANES:(c + 1) * LANES]
                    for c in range(npc)]
        return out

    def softmax_update(slot, bias_rows, m, l):
        m_out, l_out, alpha = [], [], []
        for i, (h, c) in enumerate(pieces):
            cols = slice(c * LANES, (c + 1) * LANES)
            s = s_buf[slot, h, :, cols]
            b = bias_rows[h][:, cols]
            m_new = jnp.maximum(m[i], jnp.max(s, axis=0, keepdims=True) + b)
            a = jnp.exp(m[i] - m_new)
            p = jnp.exp(s - (m_new - b))
            p_buf[slot, h, :, cols] = p.astype(BF16)
            m_out.append(m_new)
            l_out.append(a * l[i] + jnp.sum(p, axis=0, keepdims=True))
            alpha.append(a)
        return m_out, l_out, alpha

    def scores_into(slot, blk):
        kblk = kb_s[blk]
        for h in range(nh):
            s_buf[slot, h] = _dot_nt(kblk, qbs[h])

    k_own = kb_s[n]
    for h in range(nh):
        s_buf[1, h] = jnp.where(ki <= qi, _dot_nt(k_own, qbs[h]), MASK_VALUE)
    scores_into(0, 0)
    no_bias = [jnp.zeros((1, L), F32)] * nh
    m, l, alpha = softmax_update(1, no_bias, [jnp.full((1, LANES), -jnp.inf, F32)] * len(pieces),
                                 [jnp.zeros((1, LANES), F32)] * len(pieces))
    acc = [jnp.zeros((ATT_HD, LANES), F32)] * len(pieces)

    def stage(prev_blk, blk, next_blk, cur, oth, m, l, alpha, acc):
        acc = pv_update(prev_blk, oth, alpha, acc)
        scores_into(oth, next_blk)
        m, l, alpha = softmax_update(cur, [bias_s[h, pl.ds(blk, 1), :] for h in range(nh)], m, l)
        return m, l, alpha, acc

    def double_step(t, carry):
        m, l, alpha, acc = (list(c) for c in carry)
        j = 2 * t
        m, l, alpha, acc = stage(jnp.where(t == 0, n, j - 1), j, j + 1, 0, 1, m, l, alpha, acc)
        m, l, alpha, acc = stage(j, j + 1, jnp.minimum(j + 2, nb - 1), 1, 0, m, l, alpha, acc)
        return tuple(m), tuple(l), tuple(alpha), tuple(acc)

    n_double = (n + 1) // 2
    m, l, alpha, acc = lax.fori_loop(0, n_double, double_step,
                                     (tuple(m), tuple(l), tuple(alpha), tuple(acc)))
    acc = pv_update(jnp.where(n == 0, n, 2 * n_double - 1), 1, alpha, acc)
    out_t = jnp.concatenate(
        [jnp.concatenate([acc[h * npc + c] / l[h * npc + c] for c in range(npc)], axis=1)
         for h in range(nh)], axis=0)
    oc_ref[q_rows, :] = out_t.T.astype(oc_ref.dtype)


def _moba_prompt(p3, layer, depth, kt_all=None, vt_all=None):
    B, S, _ = p3.shape
    nb = S // MOBA_BLOCK
    npair = ATT_HEADS // HEADS_PER_LANE_TILE
    col = lambda off: pl.BlockSpec((None, S, LANES), lambda b, g, o=off // LANES: (b, 0, o + g))
    kv_out = pl.BlockSpec((None, None, LANES, S), lambda b, g: (b, layer, g, 0))
    carried = [] if kt_all is None else [kt_all, vt_all]
    return pl.pallas_call(
        functools.partial(_moba_prompt_body, nb, len(carried)),
        grid=(B, npair),
        in_specs=[col(OFF_QC), col(OFF_KC), col(OFF_VC)]
        + [pl.BlockSpec(memory_space=pl.ANY)] * len(carried),
        out_specs=[pl.BlockSpec((None, S, LANES), lambda b, g: (b, 0, g)), kv_out, kv_out],
        out_shape=[jax.ShapeDtypeStruct((B, S, ATT_WIDTH), BF16),
                   jax.ShapeDtypeStruct((B, depth, ATT_WIDTH, S), F32),
                   jax.ShapeDtypeStruct((B, depth, ATT_WIDTH, S), F32)],
        input_output_aliases={3: 1, 4: 2} if carried else {},
        scratch_shapes=[pltpu.VMEM((nb, MOBA_BLOCK, LANES), BF16),
                        pltpu.VMEM((nb, LANES, MOBA_BLOCK), BF16),
                        pltpu.VMEM((nb, LANES), F32),
                        pltpu.VMEM((2 * HEADS_PER_LANE_TILE * nb, LANES), BF16),
                        pltpu.VMEM((HEADS_PER_LANE_TILE, nb, MOBA_BLOCK), F32),
                        pltpu.VMEM((2, HEADS_PER_LANE_TILE, MOBA_BLOCK, MOBA_BLOCK), F32),
                        pltpu.VMEM((2, HEADS_PER_LANE_TILE, MOBA_BLOCK, MOBA_BLOCK), BF16)],
        compiler_params=_params(2),
        name="moba_prompt",
    )(p3, p3, p3, *carried)


PAGES_PER_STEP = 16


def _moba_select_body(n_blocks, pages_per_block, pt_ref, q_ref, *refs):
    pages = refs[:PAGES_PER_STEP]
    idx_ref = refs[PAGES_PER_STEP]
    qcol_s, gate_s = refs[PAGES_PER_STEP + 1:]
    g = pl.program_id(1)
    blocks_per_step = PAGES_PER_STEP // pages_per_block

    @pl.when(g == 0)
    def _():
        q = q_ref[...]
        for c in range(ATT_WIDTH // LANES):
            qcol_s[c * LANES:(c + 1) * LANES, :] = _as_column(q[:, c * LANES:(c + 1) * LANES])
        gate_s[...] = jnp.zeros_like(gate_s)

    q_col = qcol_s[...].reshape(ATT_HEADS, ATT_HD, LANES)
    lane = lax.broadcasted_iota(jnp.int32, gate_s.shape, 1)
    acc = gate_s[...]
    for blk in range(blocks_per_step):
        rows = jnp.sum(pages[blk * pages_per_block][...] * q_col, axis=1)
        for i in range(1, pages_per_block):
            rows = rows + jnp.sum(pages[blk * pages_per_block + i][...] * q_col, axis=1)
        col = jnp.sum(rows, axis=1, keepdims=True)
        acc = jnp.where(lane == g * blocks_per_step + blk, col, acc)
    gate_s[...] = acc

    @pl.when(g == pl.num_programs(1) - 1)
    def _():
        gate = acc * (1.0 / MOBA_BLOCK)
        ln = lax.broadcasted_iota(jnp.int32, gate.shape, 1)
        lnf = ln.astype(F32)
        gate = jnp.where(ln < n_blocks, gate, MASK_VALUE)
        out = jnp.zeros(gate.shape, F32)
        for r in range(MOBA_TOPK):
            top = jnp.max(gate, axis=1, keepdims=True)
            ix = jnp.min(jnp.where(gate == top, lnf, float(LANES)), axis=1, keepdims=True)
            out = jnp.where(ln == r, ix, out)
            gate = jnp.where(lnf == ix, -jnp.inf, gate)
        idx_ref[...] = out.astype(jnp.int32)


def _moba_select(ck_t, page_table, q3, layer):
    NS, n_pages = page_table.shape
    page = ck_t.shape[-1]
    pages_per_block = MOBA_BLOCK // page
    n_blocks = n_pages // pages_per_block
    assert n_blocks <= LANES and n_pages % PAGES_PER_STEP == 0 and page == LANES
    page_spec = lambda i: pl.BlockSpec(
        (None, None, ATT_HEADS, ATT_HD, page),
        lambda b, g, pt, i=i: (pt[b, g * PAGES_PER_STEP + i], layer, 0, 0, 0))
    grid_spec = pltpu.PrefetchScalarGridSpec(
        num_scalar_prefetch=1,
        grid=(NS, n_pages // PAGES_PER_STEP),
        in_specs=[pl.BlockSpec((None, 1, ATT_WIDTH), lambda b, g, pt: (b, 0, 0))]
        + [page_spec(i) for i in range(PAGES_PER_STEP)],
        out_specs=pl.BlockSpec((None, ATT_HEADS, LANES), lambda b, g, pt: (b, 0, 0)),
        scratch_shapes=[pltpu.VMEM((ATT_WIDTH, LANES), F32), pltpu.VMEM((ATT_HEADS, LANES), F32)])
    return pl.pallas_call(
        functools.partial(_moba_select_body, n_blocks, pages_per_block),
        grid_spec=grid_spec,
        out_shape=jax.ShapeDtypeStruct((NS, ATT_HEADS, LANES), jnp.int32),
        compiler_params=_params(2),
        name="moba_select",
    )(page_table, q3, *([ck_t] * PAGES_PER_STEP))


ATTEND_HEADS_PER_STEP = 4


def _moba_sample_attend_body(n_tiles, pt_ref, ix_ref, q_ref, kn_ref, vn_ref, *refs):
    hps = ATTEND_HEADS_PER_STEP
    o_ref = refs[2 * hps * n_tiles]
    scale = ATT_HD ** -0.5
    for hh in range(hps):
        k_tiles = refs[hh * n_tiles:(hh + 1) * n_tiles]
        v_tiles = refs[(hps + hh) * n_tiles:(hps + hh + 1) * n_tiles]
        q = q_ref[hh]
        qb = jnp.broadcast_to(q * scale, (8, ATT_HD)).astype(BF16)
        scores = [_dot(qb, kt[...].astype(BF16))[0:1] for kt in k_tiles]
        s_own = jnp.sum(q * kn_ref[hh], axis=1, keepdims=True) * scale
        m = s_own
        for s in scores:
            m = jnp.maximum(m, jnp.max(s, axis=1, keepdims=True))
        p_own = jnp.exp(s_own - m)
        l = p_own
        o = p_own * vn_ref[hh]
        for s, vt in zip(scores, v_tiles):
            p = jnp.exp(s - m)
            l = l + jnp.sum(p, axis=1, keepdims=True)
            pb = jnp.broadcast_to(p, (8, p.shape[1])).astype(BF16)
            o = o + _dot_nt(pb, vt[...].astype(BF16))[0:1]
        o_ref[hh] = (o / l).astype(o_ref.dtype)


def _moba_sample_attend(ck_t, cv_t, page_table, idx_flat, q4, kn4, vn4, layer):
    NS = page_table.shape[0]
    page = ck_t.shape[-1]
    pages_per_block = MOBA_BLOCK // page
    n_tiles = MOBA_TOPK * pages_per_block
    hps = ATTEND_HEADS_PER_STEP

    def tile_spec(hh, i):
        r, pg = divmod(i, pages_per_block)

        def index(b, hb, pt, ix):
            h = hb * hps + hh
            blk = ix[(b * ATT_HEADS + h) * MOBA_TOPK + r]
            return pt[b, blk * pages_per_block + pg], layer, h, 0, 0

        return pl.BlockSpec((None, None, None, ATT_HD, page), index)

    tiles = [tile_spec(hh, i) for hh in range(hps) for i in range(n_tiles)]
    vec = pl.BlockSpec((None, hps, 1, ATT_HD), lambda b, hb, pt, ix: (b, hb, 0, 0))
    grid_spec = pltpu.PrefetchScalarGridSpec(
        num_scalar_prefetch=2,
        grid=(NS, ATT_HEADS // hps),
        in_specs=[vec, vec, vec] + tiles * 2,
        out_specs=vec)
    return pl.pallas_call(
        functools.partial(_moba_sample_attend_body, n_tiles),
        grid_spec=grid_spec,
        out_shape=jax.ShapeDtypeStruct((NS, ATT_HEADS, 1, ATT_HD), BF16),
        compiler_params=_params(2),
        name="moba_sample_attend",
    )(page_table, idx_flat, q4, kn4, vn4, *([ck_t] * len(tiles)), *([cv_t] * len(tiles)))


def _merge_body(oa_ref, ob_ref, oc_ref, ga_ref, gb_ref, gc_ref, ba_ref, bb_ref, bc_ref,
                x_ref, wa_ref, wb_ref, wc_ref, wo_ref, out_ref):
    mix = (jax.nn.sigmoid(ga_ref[...] + ba_ref[...]) * _dot(oa_ref[...], wa_ref[...])
           + jax.nn.sigmoid(gb_ref[...] + bb_ref[...]) * _dot(ob_ref[...], wb_ref[...])
           + jax.nn.sigmoid(gc_ref[...] + bc_ref[...]) * _dot(oc_ref[...], wc_ref[...]))
    out_ref[...] = x_ref[...] + _dot(mix.astype(BF16), wo_ref[...])


def _merge(oa, ob, oc, p, b_gate, x, wa, wb, wc, wo, tm):
    T, D = x.shape
    W = oa.shape[1]
    br = pl.BlockSpec((tm, W), lambda i: (i, 0))
    gate = lambda k: pl.BlockSpec((tm, D), lambda i, k=k: (i, OFF_GATE // D + k))
    bias = lambda k: pl.BlockSpec((1, D), lambda i, k=k: (0, k))
    wbr = pl.BlockSpec((W, D), lambda i: (0, 0))
    return pl.pallas_call(
        _merge_body,
        grid=(T // tm,),
        in_specs=[br, br, br, gate(0), gate(1), gate(2), bias(0), bias(1), bias(2),
                  pl.BlockSpec((tm, D), lambda i: (i, 0)), wbr, wbr, wbr,
                  pl.BlockSpec((D, D), lambda i: (0, 0))],
        out_specs=pl.BlockSpec((tm, D), lambda i: (i, 0)),
        out_shape=jax.ShapeDtypeStruct((T, D), F32),
        compiler_params=_params(1),
        name="merge",
    )(oa, ob, oc, p, p, p, b_gate, b_gate, b_gate, x, wa, wb, wc, wo)


def _ffn_body(final, x_ref, n2_ref, wg_ref, wu_ref, wd_ref, fn_ref, out_ref, h_s, acc_s):
    f = pl.program_id(1)

    @pl.when(f == 0)
    def _():
        h_s[...] = (_rms(x_ref[...]) * n2_ref[...]).astype(BF16)
        acc_s[...] = jnp.zeros_like(acc_s)

    h = h_s[...]
    act = _silu(_dot(h, wg_ref[...])) * _dot(h, wu_ref[...])
    acc_s[...] += _dot(act.astype(BF16), wd_ref[...])

    @pl.when(f == pl.num_programs(1) - 1)
    def _():
        y = x_ref[...] + acc_s[...]
        if final:
            y = _rms(y) * fn_ref[...]
        out_ref[...] = y


def _ffn(x, n2, w_up, w_down, final_norm, final, tm, tf):
    T, D = x.shape
    d_ff = w_down.shape[0]
    nf = d_ff // tf
    return pl.pallas_call(
        functools.partial(_ffn_body, final),
        grid=(T // tm, nf),
        in_specs=[pl.BlockSpec((tm, D), lambda i, f: (i, 0)),
                  pl.BlockSpec((1, D), lambda i, f: (0, 0)),
                  pl.BlockSpec((D, tf), lambda i, f: (0, f)),
                  pl.BlockSpec((D, tf), lambda i, f: (0, nf + f)),
                  pl.BlockSpec((tf, D), lambda i, f: (f, 0)),
                  pl.BlockSpec((1, D), lambda i, f: (0, 0))],
        out_specs=pl.BlockSpec((tm, D), lambda i, f: (i, 0)),
        out_shape=jax.ShapeDtypeStruct((T, D), F32),
        scratch_shapes=[pltpu.VMEM((tm, D), BF16), pltpu.VMEM((tm, D), F32)],
        compiler_params=_params(2),
        name="ffn",
    )(x, n2, w_up, w_up, w_down, final_norm)


def _tiles(T):
    big = T >= 1024
    return dict(inproj_tm=1024 if big else T, inproj_tn=2048 if big else 1024,
                merge_tm=512 if big else T,
                ffn_tm=512 if big else T, ffn_tf=1408,
                mixer_tc=1024)


def kernel(x_prompt, x_sample, cache_k, cache_v, state_hgrn, state_conv, page_table, norm1, norm2, final_norm, w_in, b_gate, hg_lb_raw, hg_norm, conv_w, w_branch_a, w_branch_b, w_branch_c, w_out, w_ffn_up, w_ffn_down):
    B, S, D = x_prompt.shape
    NS = x_sample.shape[0]
    depth = w_in.shape[0]
    Tp = B * S
    tp, ts = _tiles(Tp), _tiles(NS)

    w_in_b = w_in.astype(BF16)
    wa_b, wb_b, wc_b = (w.astype(BF16) for w in (w_branch_a, w_branch_b, w_branch_c))
    wo_b, wu_b, wd_b = (w.astype(BF16) for w in (w_out, w_ffn_up, w_ffn_down))
    fnorm = final_norm[None]
    ck_t = jnp.transpose(cache_k, (0, 1, 3, 4, 2))
    cv_t = jnp.transpose(cache_v, (0, 1, 3, 4, 2))

    xp = x_prompt.reshape(Tp, D)
    xs = x_sample.reshape(NS, D)
    sp_l, cp_l, ks_l, vs_l, ss_l, cs_l = ([] for _ in range(6))
    kt_all = vt_all = None
    for l in range(depth):
        last = l == depth - 1
        n1, n2, bg = norm1[l][None], norm2[l][None], b_gate[l][None]
        gain = hg_norm[l][None]

        p = _inproj(xp, n1, w_in_b[l], tp["inproj_tm"], tp["inproj_tn"])
        p3 = p.reshape(B, S, -1)
        oa, st = _hgrn_prompt(p3, hg_lb_raw, gain, l, tp["mixer_tc"])
        ob, cbuf = _conv_prompt(p3, conv_w[l], tp["mixer_tc"])
        oc, kt_all, vt_all = _moba_prompt(p3, l, depth, kt_all, vt_all)
        x1 = _merge(oa.reshape(Tp, -1), ob.reshape(Tp, -1), oc.reshape(Tp, -1), p, bg, xp,
                    wa_b[l], wb_b[l], wc_b[l], wo_b[l], tp["merge_tm"])
        xp = _ffn(x1, n2, wu_b[l], wd_b[l], fnorm, last, tp["ffn_tm"], tp["ffn_tf"])
        sp_l.append(st)
        cp_l.append(cbuf)

        ps = _inproj(xs, n1, w_in_b[l], ts["inproj_tm"], ts["inproj_tn"])
        ps3 = ps.reshape(NS, 1, -1)
        oa_s, st_s = _hgrn_sample(ps3, state_hgrn, hg_lb_raw, gain, l)
        ob_s, z_s = _conv_sample(ps, conv_w[l], state_conv[:, l, 0], state_conv[:, l, 1])
        q_s = ps[:, OFF_QC:OFF_QC + ATT_WIDTH]
        k_s = ps[:, OFF_KC:OFF_KC + ATT_WIDTH]
        v_s = ps[:, OFF_VC:OFF_VC + ATT_WIDTH]
        sel = _moba_select(ck_t, page_table, q_s.reshape(NS, 1, ATT_WIDTH), l)
        idx = sel[:, :, :MOBA_TOPK].reshape(-1)
        to4 = lambda a: a.reshape(NS, ATT_HEADS, 1, ATT_HD)
        oc_s = _moba_sample_attend(ck_t, cv_t, page_table, idx, to4(q_s), to4(k_s), to4(v_s), l)
        x1s = _merge(oa_s.reshape(NS, -1), ob_s, oc_s.reshape(NS, -1), ps, bg, xs,
                     wa_b[l], wb_b[l], wc_b[l], wo_b[l], ts["merge_tm"])
        xs = _ffn(x1s, n2, wu_b[l], wd_b[l], fnorm, last, ts["ffn_tm"], ts["ffn_tf"])
        ks_l.append(k_s.reshape(NS, 1, ATT_HEADS, ATT_HD))
        vs_l.append(v_s.reshape(NS, 1, ATT_HEADS, ATT_HD))
        ss_l.append(st_s)
        cs_l.append(jnp.stack([state_conv[:, l, 1], z_s], axis=1))

    from_t = lambda a: jnp.transpose(a.reshape(B, depth, ATT_HEADS, ATT_HD, S), (0, 1, 4, 2, 3))
    return (xp.reshape(B, S, D), xs.reshape(NS, 1, D),
            from_t(kt_all), from_t(vt_all),
            jnp.stack(sp_l, axis=1), jnp.stack(cp_l, axis=1),
            jnp.stack(ks_l, axis=1), jnp.stack(vs_l, axis=1),
            jnp.stack(ss_l, axis=1), jnp.stack(cs_l, axis=1))
```

```python
import functools

import jax
import jax.numpy as jnp
from jax import lax
from jax.experimental import pallas as pl
from jax.experimental.pallas import tpu as pltpu

F32 = jnp.float32
BF16 = jnp.bfloat16

EPS = 1e-6
MASK_VALUE = -1e30
EXP_CLAMP = 60.0
HG_HEADS = 4
HG_DK = 128
HG_DV = 128
HG_WIDTH = HG_HEADS * HG_DK
HG_CHUNK = 32
CONV_DIM = 512
CONV_W = 3
ATT_HEADS = 8
ATT_HD = 64
ATT_WIDTH = ATT_HEADS * ATT_HD
MOBA_BLOCK = 256
MOBA_TOPK = 3
N_BRANCH = 3
LANES = 128
HEADS_PER_LANE_TILE = LANES // ATT_HD

OFF_QA, OFF_FA, OFF_IA, OFF_OGA = 0, 512, 1024, 1536
OFF_BG, OFF_CG, OFF_HC = 2048, 2560, 3072
OFF_QC, OFF_KC, OFF_VC = 3584, 4096, 4608
OFF_GATE = 5120

VMEM_LIMIT = 56 * 1024 * 1024
HIGHEST = lax.Precision.HIGHEST


def _params(n_axes):
    return pltpu.CompilerParams(dimension_semantics=("arbitrary",) * n_axes,
                                vmem_limit_bytes=VMEM_LIMIT)


def _dot(a, b):
    return jnp.dot(a, b, preferred_element_type=F32)


def _dot_nt(a, b, precision=None):
    return lax.dot_general(a, b, (((1,), (1,)), ((), ())),
                           preferred_element_type=F32, precision=precision)


def _dot_tn(a, b):
    return lax.dot_general(a, b, (((0,), (0,)), ((), ())), preferred_element_type=F32)


def _rms(x):
    return x * lax.rsqrt(jnp.mean(x * x, axis=-1, keepdims=True) + EPS)


def _silu(x):
    h = 0.5 * x
    return h + h * jnp.tanh(h)


def _log_sigmoid(a):
    return jnp.minimum(a, 0.0) - jnp.log1p(jnp.exp(-jnp.abs(a)))


def _hgrn_gate_inputs(qa, fa, lb):
    q = _silu(qa)
    logf = _log_sigmoid(fa) + jnp.log1p(lb * jnp.exp(jnp.minimum(-fa, EXP_CLAMP)))
    t = jnp.tanh(0.5 * logf)
    k = -2.0 * t / (1.0 - t)
    return q, k, logf


def _layer_lower_bound(raw, layer):
    e = jnp.exp(raw - jnp.max(raw, axis=0, keepdims=True))
    pr = e / jnp.sum(e, axis=0, keepdims=True)
    lb = jnp.zeros_like(pr[0:1])
    for i in range(1, layer + 1):
        lb = lb + pr[i:i + 1]
    return lb


def _inproj_body(x_ref, g_ref, w_ref, o_ref, h_scr):
    @pl.when(pl.program_id(1) == 0)
    def _():
        h_scr[...] = (_rms(x_ref[...]) * g_ref[...]).astype(BF16)

    o_ref[...] = _dot(h_scr[...], w_ref[...])


def _inproj(x2d, gain, w_bf, tm, tn):
    T, D = x2d.shape
    N = w_bf.shape[1]
    return pl.pallas_call(
        _inproj_body,
        grid=(T // tm, N // tn),
        in_specs=[pl.BlockSpec((tm, D), lambda i, j: (i, 0)),
                  pl.BlockSpec((1, D), lambda i, j: (0, 0)),
                  pl.BlockSpec((D, tn), lambda i, j: (0, j))],
        out_specs=pl.BlockSpec((tm, tn), lambda i, j: (i, j)),
        out_shape=jax.ShapeDtypeStruct((T, N), F32),
        scratch_shapes=[pltpu.VMEM((tm, D), BF16)],
        compiler_params=_params(2),
        name="inproj",
    )(x2d, gain, w_bf)


def _chunk_cumsum(x, chunk):
    rin = lax.broadcasted_iota(jnp.int32, x.shape, 0) % chunk
    shift = 1
    while shift < chunk:
        x = x + jnp.where(rin >= shift, pltpu.roll(x, shift, 0), 0.0)
        shift *= 2
    return x


def _hgrn_prompt_body(layer, nchunk, q_ref, f_ref, i_ref, og_ref, lbr_ref, gn_ref,
                      oa_ref, st_ref, st_s):
    C = HG_CHUNK
    t = pl.program_id(2)
    tc = q_ref.shape[0]

    @pl.when(t == 0)
    def _():
        st_s[...] = jnp.zeros_like(st_s)

    lb = _layer_lower_bound(lbr_ref[...], layer)
    q, k, logf = _hgrn_gate_inputs(q_ref[...], f_ref[...], lb)
    b = _chunk_cumsum(logf, C)

    def per_chunk_row(r):
        rows = b.reshape(nchunk, C, HG_DK)[:, r:r + 1, :]
        return jnp.broadcast_to(rows, (nchunk, C, HG_DK)).reshape(tc, HG_DK)

    b_ref = per_chunk_row(C // 2)
    b_last = per_chunk_row(C - 1)
    qx = (q * jnp.exp(b - b_ref)).astype(BF16)
    kx = (k * jnp.exp(b_ref - b)).astype(BF16)
    qd = (q * jnp.exp(b)).astype(BF16)
    kd = (k * jnp.exp(b_last - b)).astype(BF16)
    decay = jnp.exp(b_last)
    vb = i_ref[...].astype(BF16)

    row = lax.broadcasted_iota(jnp.int32, (C, C), 0)
    col = lax.broadcasted_iota(jnp.int32, (C, C), 1)
    causal = row >= col

    st = st_s[...]
    outs = []
    for c in range(nchunk):
        rows = slice(c * C, (c + 1) * C)
        a = jnp.where(causal, _dot_nt(qx[rows], kx[rows]), 0.0)
        o = _dot(a.astype(BF16), vb[rows]) + _dot_nt(qd[rows], st.astype(BF16))
        st = st * decay[c * C:c * C + 1] + _dot_tn(vb[rows], kd[rows])
        outs.append(o)
    st_s[...] = st

    o = _rms(jnp.concatenate(outs, axis=0)) * gn_ref[...]
    oa_ref[...] = (o * _silu(og_ref[...])).astype(oa_ref.dtype)

    @pl.when(t == pl.num_programs(2) - 1)
    def _():
        st_ref[...] = st.T


def _hgrn_prompt(p3, lb_raw, gain, layer, tc):
    B, S, _ = p3.shape
    depth = lb_raw.shape[0]
    blk = lambda off: pl.BlockSpec((None, tc, HG_DK),
                                   lambda b, h, t, o=off // HG_DK: (b, t, o + h))
    return pl.pallas_call(
        functools.partial(_hgrn_prompt_body, layer, tc // HG_CHUNK),
        grid=(B, HG_HEADS, S // tc),
        in_specs=[blk(OFF_QA), blk(OFF_FA), blk(OFF_IA), blk(OFF_OGA),
                  pl.BlockSpec((depth, HG_DK), lambda b, h, t: (0, h)),
                  pl.BlockSpec((1, HG_DV), lambda b, h, t: (0, 0))],
        out_specs=[pl.BlockSpec((None, tc, HG_DV), lambda b, h, t: (b, t, h)),
                   pl.BlockSpec((None, None, HG_DK, HG_DV), lambda b, h, t: (b, h, 0, 0))],
        out_shape=[jax.ShapeDtypeStruct((B, S, HG_HEADS * HG_DV), BF16),
                   jax.ShapeDtypeStruct((B, HG_HEADS, HG_DK, HG_DV), F32)],
        scratch_shapes=[pltpu.VMEM((HG_DV, HG_DK), F32)],
        compiler_params=_params(3),
        name="hgrn_prompt",
    )(p3, p3, p3, p3, lb_raw, gain)


def _as_column(row):
    n = row.shape[-1]
    return jnp.broadcast_to(row, (n, n)).T


def _hgrn_sample_body(layer, q_ref, f_ref, i_ref, og_ref, lbr_ref, gn_ref, s0_ref,
                      oa_ref, st_ref):
    lb_all = _layer_lower_bound(lbr_ref[...], layer)
    q_all, k_all, logf_all = _hgrn_gate_inputs(q_ref[...], f_ref[...], lb_all)
    v_all, og_all = i_ref[...], og_ref[...]
    for h in range(HG_HEADS):
        hs = slice(h * HG_DK, (h + 1) * HG_DK)
        s = _as_column(jnp.exp(logf_all[:, hs])) * s0_ref[h] + _as_column(k_all[:, hs]) * v_all[:, hs]
        st_ref[h] = s
        o = jnp.sum(_as_column(q_all[:, hs]) * s, axis=0, keepdims=True)
        o = _rms(o) * gn_ref[...]
        oa_ref[:, hs] = (o * _silu(og_all[:, hs])).astype(oa_ref.dtype)


def _hgrn_sample(ps3, state, lb_raw, gain, layer):
    NS = ps3.shape[0]
    depth = lb_raw.shape[0]
    blk = lambda off: pl.BlockSpec((None, 1, HG_WIDTH), lambda b, o=off // HG_WIDTH: (b, 0, o))
    return pl.pallas_call(
        functools.partial(_hgrn_sample_body, layer),
        grid=(NS,),
        in_specs=[blk(OFF_QA), blk(OFF_FA), blk(OFF_IA), blk(OFF_OGA),
                  pl.BlockSpec((depth, HG_WIDTH), lambda b: (0, 0)),
                  pl.BlockSpec((1, HG_DV), lambda b: (0, 0)),
                  pl.BlockSpec((None, None, HG_HEADS, HG_DK, HG_DV),
                               lambda b: (b, layer, 0, 0, 0))],
        out_specs=[pl.BlockSpec((None, 1, HG_HEADS * HG_DV), lambda b: (b, 0, 0)),
                   pl.BlockSpec((None, HG_HEADS, HG_DK, HG_DV), lambda b: (b, 0, 0, 0))],
        out_shape=[jax.ShapeDtypeStruct((NS, 1, HG_HEADS * HG_DV), BF16),
                   jax.ShapeDtypeStruct((NS, HG_HEADS, HG_DK, HG_DV), F32)],
        compiler_params=_params(1),
        name="hgrn_sample",
    )(ps3, ps3, ps3, ps3, lb_raw, gain, state)


CONV_PAD = 8


def _conv_prompt_body(bg_ref, cg_ref, hc_ref, w_ref, ob_ref, buf_ref, z_s):
    t = pl.program_id(1)
    tc = bg_ref.shape[0]

    @pl.when(t == 0)
    def _():
        z_s[0:CONV_PAD, :] = jnp.zeros((CONV_PAD, CONV_DIM), F32)

    z = cg_ref[...] * hc_ref[...]
    z_s[CONV_PAD:CONV_PAD + tc, :] = z
    w = w_ref[...]
    y = w[CONV_W - 1:CONV_W] * z
    for i in range(CONV_W - 1):
        shift = CONV_W - 1 - i
        y = y + w[i:i + 1] * z_s[CONV_PAD - shift:CONV_PAD - shift + tc, :]
    ob_ref[...] = (bg_ref[...] * y).astype(ob_ref.dtype)
    z_s[0:CONV_PAD, :] = z[tc - CONV_PAD:tc, :]

    @pl.when(t == pl.num_programs(1) - 1)
    def _():
        buf_ref[...] = z[tc - (CONV_W - 1):tc, :]


def _conv_prompt(p3, w, tc):
    B, S, _ = p3.shape
    blk = lambda off: pl.BlockSpec((None, tc, CONV_DIM),
                                   lambda b, t, o=off // CONV_DIM: (b, t, o))
    return pl.pallas_call(
        _conv_prompt_body,
        grid=(B, S // tc),
        in_specs=[blk(OFF_BG), blk(OFF_CG), blk(OFF_HC),
                  pl.BlockSpec((CONV_W, CONV_DIM), lambda b, t: (0, 0))],
        out_specs=[pl.BlockSpec((None, tc, CONV_DIM), lambda b, t: (b, t, 0)),
                   pl.BlockSpec((None, CONV_W - 1, CONV_DIM), lambda b, t: (b, 0, 0))],
        out_shape=[jax.ShapeDtypeStruct((B, S, CONV_DIM), BF16),
                   jax.ShapeDtypeStruct((B, CONV_W - 1, CONV_DIM), F32)],
        scratch_shapes=[pltpu.VMEM((tc + CONV_PAD, CONV_DIM), F32)],
        compiler_params=_params(2),
        name="conv_prompt",
    )(p3, p3, p3, w)


def _conv_sample_body(bg_ref, cg_ref, hc_ref, w_ref, b0_ref, b1_ref, ob_ref, z_ref):
    z = cg_ref[...] * hc_ref[...]
    w = w_ref[...]
    y = w[0:1] * b0_ref[...] + w[1:2] * b1_ref[...] + w[2:3] * z
    ob_ref[...] = (bg_ref[...] * y).astype(ob_ref.dtype)
    z_ref[...] = z


def _conv_sample(ps, w, buf0, buf1):
    NS = ps.shape[0]
    blk = lambda off: pl.BlockSpec((NS, CONV_DIM), lambda i, o=off // CONV_DIM: (0, o))
    full = pl.BlockSpec((NS, CONV_DIM), lambda i: (0, 0))
    return pl.pallas_call(
        _conv_sample_body,
        grid=(1,),
        in_specs=[blk(OFF_BG), blk(OFF_CG), blk(OFF_HC),
                  pl.BlockSpec((CONV_W, CONV_DIM), lambda i: (0, 0)), full, full],
        out_specs=[full, full],
        out_shape=[jax.ShapeDtypeStruct((NS, CONV_DIM), BF16),
                   jax.ShapeDtypeStruct((NS, CONV_DIM), F32)],
        compiler_params=_params(1),
        name="conv_sample",
    )(ps, ps, ps, w, buf0, buf1)


def _moba_prompt_body(nb, n_carried, q_ref, k_ref, v_ref, *refs):
    (oc_ref, kt_ref, vt_ref,
     kb_s, vt_s, mean_s, msplit_s, bias_s, qb_s, s_buf, p_buf) = refs[n_carried:]
    L = MOBA_BLOCK
    nh = HEADS_PER_LANE_TILE

    for blk in range(nb):
        rows = slice(blk * L, (blk + 1) * L)
        kblk = k_ref[rows, :]
        kb_s[blk] = kblk.astype(BF16)
        mean_s[blk:blk + 1, :] = jnp.mean(kblk, axis=0, keepdims=True)
        kt_ref[:, rows] = kblk.T
        vt = v_ref[rows, :].T
        vt_ref[:, rows] = vt
        vt_s[blk] = vt.astype(BF16)
    mean = mean_s[...]
    mean_head = lax.broadcasted_iota(jnp.int32, mean.shape, 1) // ATT_HD
    for h in range(nh):
        mh = jnp.where(mean_head == h, mean, 0.0)
        hi = mh.astype(BF16)
        msplit_s[h * nb:(h + 1) * nb, :] = hi
        msplit_s[(nh + h) * nb:(nh + h + 1) * nb, :] = (mh - hi.astype(F32)).astype(BF16)

    npc = L // LANES
    pieces = [(h, c) for h in range(nh) for c in range(npc)]
    blk_id = lax.broadcasted_iota(jnp.int32, (nb, L), 0)
    lane_head = lax.broadcasted_iota(jnp.int32, (L, LANES), 1) // ATT_HD
    causal = (lax.broadcasted_iota(jnp.int32, (L, L), 0)
              <= lax.broadcasted_iota(jnp.int32, (L, L), 1))

    def q_rows(n):
        return pl.ds(pl.multiple_of(n * L, L), L)

    def pv_update(blk, slot, alpha, acc):
        vt = vt_s[blk]
        out = []
        for h in range(nh):
            pv = _dot(vt[h * ATT_HD:(h + 1) * ATT_HD], p_buf[slot, h])
            out += [alpha[h * npc + c] * acc[h * npc + c] + pv[:, c * LANES:(c + 1) * LANES]
                    for c in range(npc)]
        return out

    def softmax_update(slot, bias_rows, m, l):
        m_out, l_out, alpha = [], [], []
        for i, (h, c) in enumerate(pieces):
            cols = slice(c * LANES, (c + 1) * LANES)
            s = s_buf[slot, h, :, cols]
            b = bias_rows[h][:, cols]
            m_new = jnp.maximum(m[i], jnp.max(s, axis=0, keepdims=True) + b)
            a = jnp.exp(m[i] - m_new)
            p = jnp.exp(s - (m_new - b))
            p_buf[slot, h, :, cols] = p.astype(BF16)
            m_out.append(m_new)
            l_out.append(a * l[i] + jnp.sum(p, axis=0, keepdims=True))
            alpha.append(a)
        return m_out, l_out, alpha

    def scores_into(slot, blk):
        kblk = kb_s[blk]
        for h in range(nh):
            s_buf[slot, h] = _dot_nt(kblk, qb_s[h])

    def prepare(n):
        qf = q_ref[q_rows(n), :]
        q_hi = qf.astype(BF16)
        q_lo = (qf - q_hi.astype(F32)).astype(BF16)
        g_hi = _dot_nt(msplit_s[...], q_hi)
        gates = g_hi[:nh * nb] + g_hi[nh * nb:] + _dot_nt(msplit_s[:nh * nb, :], q_lo)
        for h in range(nh):
            qh = jnp.where(lane_head == h, qf, 0.0)
            qb_s[h] = (qh * (ATT_HD ** -0.5)).astype(BF16)
            gate = jnp.where(blk_id < n, gates[h * nb:(h + 1) * nb], MASK_VALUE)
            cnt = jnp.zeros((nb, L), jnp.int32)
            for jp in range(nb):
                gj = gate[jp:jp + 1, :]
                beats = jnp.where(gj > gate, 1, jnp.where((gj == gate) & (blk_id > jp), 1, 0))
                cnt = cnt + beats
            bias_s[h] = jnp.where((cnt < MOBA_TOPK) & (blk_id < n), 0.0, MASK_VALUE)
        k_own = kb_s[n]
        for h in range(nh):
            s_buf[1, h] = jnp.where(causal, _dot_nt(k_own, qb_s[h]), MASK_VALUE)
        scores_into(0, 0)
        m, l, alpha = softmax_update(1, [jnp.zeros((1, L), F32)] * nh,
                                     [jnp.full((1, LANES), -jnp.inf, F32)] * len(pieces),
                                     [jnp.zeros((1, LANES), F32)] * len(pieces))
        return tuple(m), tuple(l), tuple(alpha)

    def stage(prev_blk, blk, next_blk, cur, oth, m, l, alpha, acc):
        acc = pv_update(prev_blk, oth, alpha, acc)
        if next_blk is not None:
            scores_into(oth, next_blk)
        m, l, alpha = softmax_update(cur, [bias_s[h, pl.ds(blk, 1), :] for h in range(nh)], m, l)
        return m, l, alpha, acc

    def query_block(n, state):
        m, l, alpha = (list(c) for c in state)
        acc = [jnp.zeros((ATT_HD, LANES), F32)] * len(pieces)

        def double_step(t, carry):
            m, l, alpha, acc = (list(c) for c in carry)
            j = 2 * t
            m, l, alpha, acc = stage(jnp.where(t == 0, n, j - 1), j, j + 1, 0, 1, m, l, alpha, acc)
            m, l, alpha, acc = stage(j, j + 1, jnp.minimum(j + 2, nb - 1), 1, 0, m, l, alpha, acc)
            return tuple(m), tuple(l), tuple(alpha), tuple(acc)

        def odd_tail(carry):
            m, l, alpha, acc = (list(c) for c in carry)
            m, l, alpha, acc = stage(jnp.where(n == 1, n, n - 2), n - 1, None, 0, 1, m, l, alpha, acc)
            return tuple(m), tuple(l), tuple(alpha), tuple(acc)

        carry = lax.fori_loop(0, n // 2, double_step,
                              (tuple(m), tuple(l), tuple(alpha), tuple(acc)))
        odd = (n & 1) == 1
        m, l, alpha, acc = lax.cond(odd, odd_tail, lambda c: c, carry)
        acc = pv_update(jnp.where(n == 0, n, n - 1), jnp.where(odd, 0, 1), alpha, acc)
        out_t = jnp.concatenate(
            [jnp.concatenate([acc[h * npc + c] / l[h * npc + c] for c in range(npc)], axis=1)
             for h in range(nh)], axis=0)
        oc_ref[q_rows(n), :] = out_t.T.astype(oc_ref.dtype)
        return prepare(jnp.minimum(n + 1, nb - 1))

    lax.fori_loop(0, nb, query_block, prepare(0))


def _moba_prompt(p3, layer, depth, kt_all=None, vt_all=None):
    B, S, _ = p3.shape
    nb = S // MOBA_BLOCK
    npair = ATT_HEADS // HEADS_PER_LANE_TILE
    col = lambda off: pl.BlockSpec((None, S, LANES), lambda b, g, o=off // LANES: (b, 0, o + g))
    kv_out = pl.BlockSpec((None, None, LANES, S), lambda b, g: (b, layer, g, 0))
    carried = [] if kt_all is None else [kt_all, vt_all]
    return pl.pallas_call(
        functools.partial(_moba_prompt_body, nb, len(carried)),
        grid=(B, npair),
        in_specs=[col(OFF_QC), col(OFF_KC), col(OFF_VC)]
        + [pl.BlockSpec(memory_space=pl.ANY)] * len(carried),
        out_specs=[pl.BlockSpec((None, S, LANES), lambda b, g: (b, 0, g)), kv_out, kv_out],
        out_shape=[jax.ShapeDtypeStruct((B, S, ATT_WIDTH), BF16),
                   jax.ShapeDtypeStruct((B, depth, ATT_WIDTH, S), F32),
                   jax.ShapeDtypeStruct((B, depth, ATT_WIDTH, S), F32)],
        input_output_aliases={3: 1, 4: 2} if carried else {},
        scratch_shapes=[pltpu.VMEM((nb, MOBA_BLOCK, LANES), BF16),
                        pltpu.VMEM((nb, LANES, MOBA_BLOCK), BF16),
                        pltpu.VMEM((nb, LANES), F32),
                        pltpu.VMEM((2 * HEADS_PER_LANE_TILE * nb, LANES), BF16),
                        pltpu.VMEM((HEADS_PER_LANE_TILE, nb, MOBA_BLOCK), F32),
                        pltpu.VMEM((HEADS_PER_LANE_TILE, MOBA_BLOCK, LANES), BF16),
                        pltpu.VMEM((2, HEADS_PER_LANE_TILE, MOBA_BLOCK, MOBA_BLOCK), F32),
                        pltpu.VMEM((2, HEADS_PER_LANE_TILE, MOBA_BLOCK, MOBA_BLOCK), BF16)],
        compiler_params=_params(2),
        name="moba_prompt",
    )(p3, p3, p3, *carried)


PAGES_PER_STEP = 16


def _moba_select_body(n_blocks, pages_per_block, pt_ref, q_ref, *refs):
    pages = refs[:PAGES_PER_STEP]
    idx_ref = refs[PAGES_PER_STEP]
    qcol_s, gate_s = refs[PAGES_PER_STEP + 1:]
    g = pl.program_id(1)
    blocks_per_step = PAGES_PER_STEP // pages_per_block

    @pl.when(g == 0)
    def _():
        q = q_ref[...]
        for c in range(ATT_WIDTH // LANES):
            qcol_s[c * LANES:(c + 1) * LANES, :] = _as_column(q[:, c * LANES:(c + 1) * LANES])
        gate_s[...] = jnp.zeros_like(gate_s)

    q_col = qcol_s[...].reshape(ATT_HEADS, ATT_HD, LANES)
    lane = lax.broadcasted_iota(jnp.int32, gate_s.shape, 1)
    acc = gate_s[...]
    for blk in range(blocks_per_step):
        rows = jnp.sum(pages[blk * pages_per_block][...] * q_col, axis=1)
        for i in range(1, pages_per_block):
            rows = rows + jnp.sum(pages[blk * pages_per_block + i][...] * q_col, axis=1)
        col = jnp.sum(rows, axis=1, keepdims=True)
        acc = jnp.where(lane == g * blocks_per_step + blk, col, acc)
    gate_s[...] = acc

    @pl.when(g == pl.num_programs(1) - 1)
    def _():
        gate = acc * (1.0 / MOBA_BLOCK)
        ln = lax.broadcasted_iota(jnp.int32, gate.shape, 1)
        lnf = ln.astype(F32)
        gate = jnp.where(ln < n_blocks, gate, MASK_VALUE)
        out = jnp.zeros(gate.shape, F32)
        for r in range(MOBA_TOPK):
            top = jnp.max(gate, axis=1, keepdims=True)
            ix = jnp.min(jnp.where(gate == top, lnf, float(LANES)), axis=1, keepdims=True)
            out = jnp.where(ln == r, ix, out)
            gate = jnp.where(lnf == ix, -jnp.inf, gate)
        idx_ref[...] = out.astype(jnp.int32)


def _moba_select(ck_t, page_table, q3, layer):
    NS, n_pages = page_table.shape
    page = ck_t.shape[-1]
    pages_per_block = MOBA_BLOCK // page
    n_blocks = n_pages // pages_per_block
    assert n_blocks <= LANES and n_pages % PAGES_PER_STEP == 0 and page == LANES
    page_spec = lambda i: pl.BlockSpec(
        (None, None, ATT_HEADS, ATT_HD, page),
        lambda b, g, pt, i=i: (pt[b, g * PAGES_PER_STEP + i], layer, 0, 0, 0))
    grid_spec = pltpu.PrefetchScalarGridSpec(
        num_scalar_prefetch=1,
        grid=(NS, n_pages // PAGES_PER_STEP),
        in_specs=[pl.BlockSpec((None, 1, ATT_WIDTH), lambda b, g, pt: (b, 0, 0))]
        + [page_spec(i) for i in range(PAGES_PER_STEP)],
        out_specs=pl.BlockSpec((None, ATT_HEADS, LANES), lambda b, g, pt: (b, 0, 0)),
        scratch_shapes=[pltpu.VMEM((ATT_WIDTH, LANES), F32), pltpu.VMEM((ATT_HEADS, LANES), F32)])
    return pl.pallas_call(
        functools.partial(_moba_select_body, n_blocks, pages_per_block),
        grid_spec=grid_spec,
        out_shape=jax.ShapeDtypeStruct((NS, ATT_HEADS, LANES), jnp.int32),
        compiler_params=_params(2),
        name="moba_select",
    )(page_table, q3, *([ck_t] * PAGES_PER_STEP))


ATTEND_HEADS_PER_STEP = 4


def _moba_sample_attend_body(n_tiles, pt_ref, ix_ref, q_ref, kn_ref, vn_ref, *refs):
    hps = ATTEND_HEADS_PER_STEP
    o_ref = refs[2 * hps * n_tiles]
    scale = ATT_HD ** -0.5
    for hh in range(hps):
        k_tiles = refs[hh * n_tiles:(hh + 1) * n_tiles]
        v_tiles = refs[(hps + hh) * n_tiles:(hps + hh + 1) * n_tiles]
        q = q_ref[hh]
        qb = jnp.broadcast_to(q * scale, (8, ATT_HD)).astype(BF16)
        scores = [_dot(qb, kt[...].astype(BF16))[0:1] for kt in k_tiles]
        s_own = jnp.sum(q * kn_ref[hh], axis=1, keepdims=True) * scale
        m = s_own
        for s in scores:
            m = jnp.maximum(m, jnp.max(s, axis=1, keepdims=True))
        p_own = jnp.exp(s_own - m)
        l = p_own
        o = p_own * vn_ref[hh]
        for s, vt in zip(scores, v_tiles):
            p = jnp.exp(s - m)
            l = l + jnp.sum(p, axis=1, keepdims=True)
            pb = jnp.broadcast_to(p, (8, p.shape[1])).astype(BF16)
            o = o + _dot_nt(pb, vt[...].astype(BF16))[0:1]
        o_ref[hh] = (o / l).astype(o_ref.dtype)


def _moba_sample_attend(ck_t, cv_t, page_table, idx_flat, q4, kn4, vn4, layer):
    NS = page_table.shape[0]
    page = ck_t.shape[-1]
    pages_per_block = MOBA_BLOCK // page
    n_tiles = MOBA_TOPK * pages_per_block
    hps = ATTEND_HEADS_PER_STEP

    def tile_spec(hh, i):
        r, pg = divmod(i, pages_per_block)

        def index(b, hb, pt, ix):
            h = hb * hps + hh
            blk = ix[(b * ATT_HEADS + h) * MOBA_TOPK + r]
            return pt[b, blk * pages_per_block + pg], layer, h, 0, 0

        return pl.BlockSpec((None, None, None, ATT_HD, page), index)

    tiles = [tile_spec(hh, i) for hh in range(hps) for i in range(n_tiles)]
    vec = pl.BlockSpec((None, hps, 1, ATT_HD), lambda b, hb, pt, ix: (b, hb, 0, 0))
    grid_spec = pltpu.PrefetchScalarGridSpec(
        num_scalar_prefetch=2,
        grid=(NS, ATT_HEADS // hps),
        in_specs=[vec, vec, vec] + tiles * 2,
        out_specs=vec)
    return pl.pallas_call(
        functools.partial(_moba_sample_attend_body, n_tiles),
        grid_spec=grid_spec,
        out_shape=jax.ShapeDtypeStruct((NS, ATT_HEADS, 1, ATT_HD), BF16),
        compiler_params=_params(2),
        name="moba_sample_attend",
    )(page_table, idx_flat, q4, kn4, vn4, *([ck_t] * len(tiles)), *([cv_t] * len(tiles)))


def _merge_body(oa_ref, ob_ref, oc_ref, ga_ref, gb_ref, gc_ref, ba_ref, bb_ref, bc_ref,
                x_ref, wa_ref, wb_ref, wc_ref, wo_ref, out_ref):
    mix = (jax.nn.sigmoid(ga_ref[...] + ba_ref[...]) * _dot(oa_ref[...], wa_ref[...])
           + jax.nn.sigmoid(gb_ref[...] + bb_ref[...]) * _dot(ob_ref[...], wb_ref[...])
           + jax.nn.sigmoid(gc_ref[...] + bc_ref[...]) * _dot(oc_ref[...], wc_ref[...]))
    out_ref[...] = x_ref[...] + _dot(mix.astype(BF16), wo_ref[...])


def _merge(oa, ob, oc, p, b_gate, x, wa, wb, wc, wo, tm):
    T, D = x.shape
    W = oa.shape[1]
    br = pl.BlockSpec((tm, W), lambda i: (i, 0))
    gate = lambda k: pl.BlockSpec((tm, D), lambda i, k=k: (i, OFF_GATE // D + k))
    bias = lambda k: pl.BlockSpec((1, D), lambda i, k=k: (0, k))
    wbr = pl.BlockSpec((W, D), lambda i: (0, 0))
    return pl.pallas_call(
        _merge_body,
        grid=(T // tm,),
        in_specs=[br, br, br, gate(0), gate(1), gate(2), bias(0), bias(1), bias(2),
                  pl.BlockSpec((tm, D), lambda i: (i, 0)), wbr, wbr, wbr,
                  pl.BlockSpec((D, D), lambda i: (0, 0))],
        out_specs=pl.BlockSpec((tm, D), lambda i: (i, 0)),
        out_shape=jax.ShapeDtypeStruct((T, D), F32),
        compiler_params=_params(1),
        name="merge",
    )(oa, ob, oc, p, p, p, b_gate, b_gate, b_gate, x, wa, wb, wc, wo)


def _ffn_body(final, x_ref, n2_ref, wg_ref, wu_ref, wd_ref, fn_ref, out_ref, h_s, acc_s):
    f = pl.program_id(1)

    @pl.when(f == 0)
    def _():
        h_s[...] = (_rms(x_ref[...]) * n2_ref[...]).astype(BF16)
        acc_s[...] = jnp.zeros_like(acc_s)

    h = h_s[...]
    act = _silu(_dot(h, wg_ref[...])) * _dot(h, wu_ref[...])
    acc_s[...] += _dot(act.astype(BF16), wd_ref[...])

    @pl.when(f == pl.num_programs(1) - 1)
    def _():
        y = x_ref[...] + acc_s[...]
        if final:
            y = _rms(y) * fn_ref[...]
        out_ref[...] = y


def _ffn(x, n2, w_up, w_down, final_norm, final, tm, tf):
    T, D = x.shape
    d_ff = w_down.shape[0]
    nf = d_ff // tf
    return pl.pallas_call(
        functools.partial(_ffn_body, final),
        grid=(T // tm, nf),
        in_specs=[pl.BlockSpec((tm, D), lambda i, f: (i, 0)),
                  pl.BlockSpec((1, D), lambda i, f: (0, 0)),
                  pl.BlockSpec((D, tf), lambda i, f: (0, f)),
                  pl.BlockSpec((D, tf), lambda i, f: (0, nf + f)),
                  pl.BlockSpec((tf, D), lambda i, f: (f, 0)),
                  pl.BlockSpec((1, D), lambda i, f: (0, 0))],
        out_specs=pl.BlockSpec((tm, D), lambda i, f: (i, 0)),
        out_shape=jax.ShapeDtypeStruct((T, D), F32),
        scratch_shapes=[pltpu.VMEM((tm, D), BF16), pltpu.VMEM((tm, D), F32)],
        compiler_params=_params(2),
        name="ffn",
    )(x, n2, w_up, w_up, w_down, final_norm)


def _tiles(T):
    big = T >= 1024
    return dict(inproj_tm=1024 if big else T, inproj_tn=2048 if big else 1024,
                merge_tm=512 if big else T,
                ffn_tm=512 if big else T, ffn_tf=1408,
                mixer_tc=1024)


def kernel(x_prompt, x_sample, cache_k, cache_v, state_hgrn, state_conv, page_table, norm1, norm2, final_norm, w_in, b_gate, hg_lb_raw, hg_norm, conv_w, w_branch_a, w_branch_b, w_branch_c, w_out, w_ffn_up, w_ffn_down):
    B, S, D = x_prompt.shape
    NS = x_sample.shape[0]
    depth = w_in.shape[0]
    Tp = B * S
    tp, ts = _tiles(Tp), _tiles(NS)

    w_in_b = w_in.astype(BF16)
    wa_b, wb_b, wc_b = (w.astype(BF16) for w in (w_branch_a, w_branch_b, w_branch_c))
    wo_b, wu_b, wd_b = (w.astype(BF16) for w in (w_out, w_ffn_up, w_ffn_down))
    fnorm = final_norm[None]
    ck_t = jnp.transpose(cache_k, (0, 1, 3, 4, 2))
    cv_t = jnp.transpose(cache_v, (0, 1, 3, 4, 2))

    xp = x_prompt.reshape(Tp, D)
    xs = x_sample.reshape(NS, D)
    sp_l, cp_l, ks_l, vs_l, ss_l, cs_l = ([] for _ in range(6))
    kt_all = vt_all = None
    for l in range(depth):
        last = l == depth - 1
        n1, n2, bg = norm1[l][None], norm2[l][None], b_gate[l][None]
        gain = hg_norm[l][None]

        p = _inproj(xp, n1, w_in_b[l], tp["inproj_tm"], tp["inproj_tn"])
        p3 = p.reshape(B, S, -1)
        oa, st = _hgrn_prompt(p3, hg_lb_raw, gain, l, tp["mixer_tc"])
        ob, cbuf = _conv_prompt(p3, conv_w[l], tp["mixer_tc"])
        oc, kt_all, vt_all = _moba_prompt(p3, l, depth, kt_all, vt_all)
        x1 = _merge(oa.reshape(Tp, -1), ob.reshape(Tp, -1), oc.reshape(Tp, -1), p, bg, xp,
                    wa_b[l], wb_b[l], wc_b[l], wo_b[l], tp["merge_tm"])
        xp = _ffn(x1, n2, wu_b[l], wd_b[l], fnorm, last, tp["ffn_tm"], tp["ffn_tf"])
        sp_l.append(st)
        cp_l.append(cbuf)

        ps = _inproj(xs, n1, w_in_b[l], ts["inproj_tm"], ts["inproj_tn"])
        ps3 = ps.reshape(NS, 1, -1)
        oa_s, st_s = _hgrn_sample(ps3, state_hgrn, hg_lb_raw, gain, l)
        ob_s, z_s = _conv_sample(ps, conv_w[l], state_conv[:, l, 0], state_conv[:, l, 1])
        q_s = ps[:, OFF_QC:OFF_QC + ATT_WIDTH]
        k_s = ps[:, OFF_KC:OFF_KC + ATT_WIDTH]
        v_s = ps[:, OFF_VC:OFF_VC + ATT_WIDTH]
        sel = _moba_select(ck_t, page_table, q_s.reshape(NS, 1, ATT_WIDTH), l)
        idx = sel[:, :, :MOBA_TOPK].reshape(-1)
        to4 = lambda a: a.reshape(NS, ATT_HEADS, 1, ATT_HD)
        oc_s = _moba_sample_attend(ck_t, cv_t, page_table, idx, to4(q_s), to4(k_s), to4(v_s), l)
        x1s = _merge(oa_s.reshape(NS, -1), ob_s, oc_s.reshape(NS, -1), ps, bg, xs,
                     wa_b[l], wb_b[l], wc_b[l], wo_b[l], ts["merge_tm"])
        xs = _ffn(x1s, n2, wu_b[l], wd_b[l], fnorm, last, ts["ffn_tm"], ts["ffn_tf"])
        ks_l.append(k_s.reshape(NS, 1, ATT_HEADS, ATT_HD))
        vs_l.append(v_s.reshape(NS, 1, ATT_HEADS, ATT_HD))
        ss_l.append(st_s)
        cs_l.append(jnp.stack([state_conv[:, l, 1], z_s], axis=1))

    from_t = lambda a: jnp.transpose(a.reshape(B, depth, ATT_HEADS, ATT_HD, S), (0, 1, 4, 2, 3))
    return (xp.reshape(B, S, D), xs.reshape(NS, 1, D),
            from_t(kt_all), from_t(vt_all),
            jnp.stack(sp_l, axis=1), jnp.stack(cp_l, axis=1),
            jnp.stack(ks_l, axis=1), jnp.stack(vs_l, axis=1),
            jnp.stack(ss_l, axis=1), jnp.stack(cs_l, axis=1))
```

```python
import functools

import jax
import jax.numpy as jnp
from jax import lax
from jax.experimental import pallas as pl
from jax.experimental.pallas import tpu as pltpu

F32 = jnp.float32
BF16 = jnp.bfloat16

EPS = 1e-6
MASK_VALUE = -1e30
EXP_CLAMP = 60.0
HG_HEADS = 4
HG_DK = 128
HG_DV = 128
HG_WIDTH = HG_HEADS * HG_DK
HG_CHUNK = 32
CONV_DIM = 512
CONV_W = 3
ATT_HEADS = 8
ATT_HD = 64
ATT_WIDTH = ATT_HEADS * ATT_HD
MOBA_BLOCK = 256
MOBA_TOPK = 3
N_BRANCH = 3
LANES = 128
HEADS_PER_LANE_TILE = LANES // ATT_HD

OFF_QA, OFF_FA, OFF_IA, OFF_OGA = 0, 512, 1024, 1536
OFF_BG, OFF_CG, OFF_HC = 2048, 2560, 3072
OFF_QC, OFF_KC, OFF_VC = 3584, 4096, 4608
OFF_GATE = 5120

VMEM_LIMIT = 56 * 1024 * 1024
HIGHEST = lax.Precision.HIGHEST


def _params(n_axes):
    return pltpu.CompilerParams(dimension_semantics=("arbitrary",) * n_axes,
                                vmem_limit_bytes=VMEM_LIMIT)


def _dot(a, b):
    return jnp.dot(a, b, preferred_element_type=F32)


def _dot_nt(a, b, precision=None):
    return lax.dot_general(a, b, (((1,), (1,)), ((), ())),
                           preferred_element_type=F32, precision=precision)


def _dot_tn(a, b):
    return lax.dot_general(a, b, (((0,), (0,)), ((), ())), preferred_element_type=F32)


def _rms(x):
    return x * lax.rsqrt(jnp.mean(x * x, axis=-1, keepdims=True) + EPS)


def _silu(x):
    h = 0.5 * x
    return h + h * jnp.tanh(h)


def _log_sigmoid(a):
    return jnp.minimum(a, 0.0) - jnp.log1p(jnp.exp(-jnp.abs(a)))


def _hgrn_gate_inputs(qa, fa, lb):
    q = _silu(qa)
    logf = _log_sigmoid(fa) + jnp.log1p(lb * jnp.exp(jnp.minimum(-fa, EXP_CLAMP)))
    t = jnp.tanh(0.5 * logf)
    k = -2.0 * t / (1.0 - t)
    return q, k, logf


def _layer_lower_bound(raw, layer):
    e = jnp.exp(raw - jnp.max(raw, axis=0, keepdims=True))
    pr = e / jnp.sum(e, axis=0, keepdims=True)
    lb = jnp.zeros_like(pr[0:1])
    for i in range(1, layer + 1):
        lb = lb + pr[i:i + 1]
    return lb


def _inproj_body(x_ref, g_ref, w_ref, o_ref, h_scr):
    @pl.when(pl.program_id(1) == 0)
    def _():
        h_scr[...] = (_rms(x_ref[...]) * g_ref[...]).astype(BF16)

    o_ref[...] = _dot(h_scr[...], w_ref[...])


def _inproj(x2d, gain, w_bf, tm, tn):
    T, D = x2d.shape
    N = w_bf.shape[1]
    return pl.pallas_call(
        _inproj_body,
        grid=(T // tm, N // tn),
        in_specs=[pl.BlockSpec((tm, D), lambda i, j: (i, 0)),
                  pl.BlockSpec((1, D), lambda i, j: (0, 0)),
                  pl.BlockSpec((D, tn), lambda i, j: (0, j))],
        out_specs=pl.BlockSpec((tm, tn), lambda i, j: (i, j)),
        out_shape=jax.ShapeDtypeStruct((T, N), F32),
        scratch_shapes=[pltpu.VMEM((tm, D), BF16)],
        compiler_params=_params(2),
        name="inproj",
    )(x2d, gain, w_bf)


def _chunk_cumsum(x, chunk):
    rin = lax.broadcasted_iota(jnp.int32, x.shape, 0) % chunk
    shift = 1
    while shift < chunk:
        x = x + jnp.where(rin >= shift, pltpu.roll(x, shift, 0), 0.0)
        shift *= 2
    return x


def _hgrn_prompt_body(layer, nchunk, q_ref, f_ref, i_ref, og_ref, lbr_ref, gn_ref,
                      oa_ref, st_ref, st_s):
    C = HG_CHUNK
    t = pl.program_id(2)
    tc = q_ref.shape[0]

    @pl.when(t == 0)
    def _():
        st_s[...] = jnp.zeros_like(st_s)

    lb = _layer_lower_bound(lbr_ref[...], layer)
    q, k, logf = _hgrn_gate_inputs(q_ref[...], f_ref[...], lb)
    b = _chunk_cumsum(logf, C)

    def per_chunk_row(r):
        rows = b.reshape(nchunk, C, HG_DK)[:, r:r + 1, :]
        return jnp.broadcast_to(rows, (nchunk, C, HG_DK)).reshape(tc, HG_DK)

    b_ref = per_chunk_row(C // 2)
    b_last = per_chunk_row(C - 1)
    qx = (q * jnp.exp(b - b_ref)).astype(BF16)
    kx = (k * jnp.exp(b_ref - b)).astype(BF16)
    qd = (q * jnp.exp(b)).astype(BF16)
    kd = (k * jnp.exp(b_last - b)).astype(BF16)
    decay = jnp.exp(b_last)
    vb = i_ref[...].astype(BF16)

    row = lax.broadcasted_iota(jnp.int32, (C, C), 0)
    col = lax.broadcasted_iota(jnp.int32, (C, C), 1)
    causal = row >= col

    st = st_s[...]
    outs = []
    for c in range(nchunk):
        rows = slice(c * C, (c + 1) * C)
        a = jnp.where(causal, _dot_nt(qx[rows], kx[rows]), 0.0)
        o = _dot(a.astype(BF16), vb[rows]) + _dot_nt(qd[rows], st.astype(BF16))
        st = st * decay[c * C:c * C + 1] + _dot_tn(vb[rows], kd[rows])
        outs.append(o)
    st_s[...] = st

    o = _rms(jnp.concatenate(outs, axis=0)) * gn_ref[...]
    oa_ref[...] = (o * _silu(og_ref[...])).astype(oa_ref.dtype)

    @pl.when(t == pl.num_programs(2) - 1)
    def _():
        st_ref[...] = st.T


def _hgrn_prompt(p3, lb_raw, gain, layer, tc):
    B, S, _ = p3.shape
    depth = lb_raw.shape[0]
    blk = lambda off: pl.BlockSpec((None, tc, HG_DK),
                                   lambda b, h, t, o=off // HG_DK: (b, t, o + h))
    return pl.pallas_call(
        functools.partial(_hgrn_prompt_body, layer, tc // HG_CHUNK),
        grid=(B, HG_HEADS, S // tc),
        in_specs=[blk(OFF_QA), blk(OFF_FA), blk(OFF_IA), blk(OFF_OGA),
                  pl.BlockSpec((depth, HG_DK), lambda b, h, t: (0, h)),
                  pl.BlockSpec((1, HG_DV), lambda b, h, t: (0, 0))],
        out_specs=[pl.BlockSpec((None, tc, HG_DV), lambda b, h, t: (b, t, h)),
                   pl.BlockSpec((None, None, HG_DK, HG_DV), lambda b, h, t: (b, h, 0, 0))],
        out_shape=[jax.ShapeDtypeStruct((B, S, HG_HEADS * HG_DV), BF16),
                   jax.ShapeDtypeStruct((B, HG_HEADS, HG_DK, HG_DV), F32)],
        scratch_shapes=[pltpu.VMEM((HG_DV, HG_DK), F32)],
        compiler_params=_params(3),
        name="hgrn_prompt",
    )(p3, p3, p3, p3, lb_raw, gain)


def _as_column(row):
    n = row.shape[-1]
    return jnp.broadcast_to(row, (n, n)).T


def _hgrn_sample_body(layer, q_ref, f_ref, i_ref, og_ref, lbr_ref, gn_ref, s0_ref,
                      oa_ref, st_ref):
    lb_all = _layer_lower_bound(lbr_ref[...], layer)
    q_all, k_all, logf_all = _hgrn_gate_inputs(q_ref[...], f_ref[...], lb_all)
    v_all, og_all = i_ref[...], og_ref[...]
    for h in range(HG_HEADS):
        hs = slice(h * HG_DK, (h + 1) * HG_DK)
        s = _as_column(jnp.exp(logf_all[:, hs])) * s0_ref[h] + _as_column(k_all[:, hs]) * v_all[:, hs]
        st_ref[h] = s
        o = jnp.sum(_as_column(q_all[:, hs]) * s, axis=0, keepdims=True)
        o = _rms(o) * gn_ref[...]
        oa_ref[:, hs] = (o * _silu(og_all[:, hs])).astype(oa_ref.dtype)


def _hgrn_sample(ps3, state, lb_raw, gain, layer):
    NS = ps3.shape[0]
    depth = lb_raw.shape[0]
    blk = lambda off: pl.BlockSpec((None, 1, HG_WIDTH), lambda b, o=off // HG_WIDTH: (b, 0, o))
    return pl.pallas_call(
        functools.partial(_hgrn_sample_body, layer),
        grid=(NS,),
        in_specs=[blk(OFF_QA), blk(OFF_FA), blk(OFF_IA), blk(OFF_OGA),
                  pl.BlockSpec((depth, HG_WIDTH), lambda b: (0, 0)),
                  pl.BlockSpec((1, HG_DV), lambda b: (0, 0)),
                  pl.BlockSpec((None, None, HG_HEADS, HG_DK, HG_DV),
                               lambda b: (b, layer, 0, 0, 0))],
        out_specs=[pl.BlockSpec((None, 1, HG_HEADS * HG_DV), lambda b: (b, 0, 0)),
                   pl.BlockSpec((None, HG_HEADS, HG_DK, HG_DV), lambda b: (b, 0, 0, 0))],
        out_shape=[jax.ShapeDtypeStruct((NS, 1, HG_HEADS * HG_DV), BF16),
                   jax.ShapeDtypeStruct((NS, HG_HEADS, HG_DK, HG_DV), F32)],
        compiler_params=_params(1),
        name="hgrn_sample",
    )(ps3, ps3, ps3, ps3, lb_raw, gain, state)


CONV_PAD = 8


def _conv_prompt_body(bg_ref, cg_ref, hc_ref, w_ref, ob_ref, buf_ref, z_s):
    t = pl.program_id(1)
    tc = bg_ref.shape[0]

    @pl.when(t == 0)
    def _():
        z_s[0:CONV_PAD, :] = jnp.zeros((CONV_PAD, CONV_DIM), F32)

    z = cg_ref[...] * hc_ref[...]
    z_s[CONV_PAD:CONV_PAD + tc, :] = z
    w = w_ref[...]
    y = w[CONV_W - 1:CONV_W] * z
    for i in range(CONV_W - 1):
        shift = CONV_W - 1 - i
        y = y + w[i:i + 1] * z_s[CONV_PAD - shift:CONV_PAD - shift + tc, :]
    ob_ref[...] = (bg_ref[...] * y).astype(ob_ref.dtype)
    z_s[0:CONV_PAD, :] = z[tc - CONV_PAD:tc, :]

    @pl.when(t == pl.num_programs(1) - 1)
    def _():
        buf_ref[...] = z[tc - (CONV_W - 1):tc, :]


def _conv_prompt(p3, w, tc):
    B, S, _ = p3.shape
    blk = lambda off: pl.BlockSpec((None, tc, CONV_DIM),
                                   lambda b, t, o=off // CONV_DIM: (b, t, o))
    return pl.pallas_call(
        _conv_prompt_body,
        grid=(B, S // tc),
        in_specs=[blk(OFF_BG), blk(OFF_CG), blk(OFF_HC),
                  pl.BlockSpec((CONV_W, CONV_DIM), lambda b, t: (0, 0))],
        out_specs=[pl.BlockSpec((None, tc, CONV_DIM), lambda b, t: (b, t, 0)),
                   pl.BlockSpec((None, CONV_W - 1, CONV_DIM), lambda b, t: (b, 0, 0))],
        out_shape=[jax.ShapeDtypeStruct((B, S, CONV_DIM), BF16),
                   jax.ShapeDtypeStruct((B, CONV_W - 1, CONV_DIM), F32)],
        scratch_shapes=[pltpu.VMEM((tc + CONV_PAD, CONV_DIM), F32)],
        compiler_params=_params(2),
        name="conv_prompt",
    )(p3, p3, p3, w)


def _conv_sample_body(bg_ref, cg_ref, hc_ref, w_ref, b0_ref, b1_ref, ob_ref, z_ref):
    z = cg_ref[...] * hc_ref[...]
    w = w_ref[...]
    y = w[0:1] * b0_ref[...] + w[1:2] * b1_ref[...] + w[2:3] * z
    ob_ref[...] = (bg_ref[...] * y).astype(ob_ref.dtype)
    z_ref[...] = z


def _conv_sample(ps, w, buf0, buf1):
    NS = ps.shape[0]
    blk = lambda off: pl.BlockSpec((NS, CONV_DIM), lambda i, o=off // CONV_DIM: (0, o))
    full = pl.BlockSpec((NS, CONV_DIM), lambda i: (0, 0))
    return pl.pallas_call(
        _conv_sample_body,
        grid=(1,),
        in_specs=[blk(OFF_BG), blk(OFF_CG), blk(OFF_HC),
                  pl.BlockSpec((CONV_W, CONV_DIM), lambda i: (0, 0)), full, full],
        out_specs=[full, full],
        out_shape=[jax.ShapeDtypeStruct((NS, CONV_DIM), BF16),
                   jax.ShapeDtypeStruct((NS, CONV_DIM), F32)],
        compiler_params=_params(1),
        name="conv_sample",
    )(ps, ps, ps, w, buf0, buf1)


def _moba_prompt_body(nb, n_carried, q_ref, k_ref, v_ref, *refs):
    (oc_ref, kt_ref, vt_ref,
     kb_s, vt_s, mean_s, msplit_s, bias_s, qb_s, s_buf, p_buf) = refs[n_carried:]
    L = MOBA_BLOCK
    nh = HEADS_PER_LANE_TILE

    for blk in range(nb):
        rows = slice(blk * L, (blk + 1) * L)
        kblk = k_ref[rows, :]
        kb_s[blk] = kblk.astype(BF16)
        mean_s[blk:blk + 1, :] = jnp.mean(kblk, axis=0, keepdims=True)
        kt_ref[:, rows] = kblk.T
        vt = v_ref[rows, :].T
        vt_ref[:, rows] = vt
        vt_s[blk] = vt.astype(BF16)
    mean = mean_s[...]
    mean_head = lax.broadcasted_iota(jnp.int32, mean.shape, 1) // ATT_HD
    for h in range(nh):
        mh = jnp.where(mean_head == h, mean, 0.0)
        hi = mh.astype(BF16)
        msplit_s[h * nb:(h + 1) * nb, :] = hi
        msplit_s[(nh + h) * nb:(nh + h + 1) * nb, :] = (mh - hi.astype(F32)).astype(BF16)

    npc = L // LANES
    pieces = [(h, c) for h in range(nh) for c in range(npc)]
    blk_id = lax.broadcasted_iota(jnp.int32, (nb, L), 0)
    lane_head = lax.broadcasted_iota(jnp.int32, (L, LANES), 1) // ATT_HD
    causal = (lax.broadcasted_iota(jnp.int32, (L, L), 0)
              <= lax.broadcasted_iota(jnp.int32, (L, L), 1))

    def q_rows(n):
        return pl.ds(pl.multiple_of(n * L, L), L)

    def pv_update(blk, slot, alpha, acc):
        vt = vt_s[blk]
        out = []
        for h in range(nh):
            p = jnp.concatenate([p_buf[slot, h, c] for c in range(npc)], axis=1)
            pv = _dot(vt[h * ATT_HD:(h + 1) * ATT_HD], p)
            out += [alpha[h * npc + c] * acc[h * npc + c] + pv[:, c * LANES:(c + 1) * LANES]
                    for c in range(npc)]
        return out

    def softmax_update(slot, bias_rows, m, l):
        m_out, l_out, alpha = [], [], []
        for i, (h, c) in enumerate(pieces):
            cols = slice(c * LANES, (c + 1) * LANES)
            s = s_buf[slot, h, c]
            b = bias_rows[h][:, cols]
            m_new = jnp.maximum(m[i], jnp.max(s, axis=0, keepdims=True) + b)
            a = jnp.exp(m[i] - m_new)
            p = jnp.exp(s - (m_new - b))
            p_buf[slot, h, c] = p.astype(BF16)
            m_out.append(m_new)
            l_out.append(a * l[i] + jnp.sum(p, axis=0, keepdims=True))
            alpha.append(a)
        return m_out, l_out, alpha

    def stage_scores(slot, h, s):
        for c in range(npc):
            s_buf[slot, h, c] = s[:, c * LANES:(c + 1) * LANES]

    def scores_into(slot, blk):
        kblk = kb_s[blk]
        for h in range(nh):
            stage_scores(slot, h, _dot_nt(kblk, qb_s[h]))

    def prepare(n):
        qf = q_ref[q_rows(n), :]
        q_hi = qf.astype(BF16)
        q_lo = (qf - q_hi.astype(F32)).astype(BF16)
        g_hi = _dot_nt(msplit_s[...], q_hi)
        gates = g_hi[:nh * nb] + g_hi[nh * nb:] + _dot_nt(msplit_s[:nh * nb, :], q_lo)
        for h in range(nh):
            qh = jnp.where(lane_head == h, qf, 0.0)
            qb_s[h] = (qh * (ATT_HD ** -0.5)).astype(BF16)
            gate = jnp.where(blk_id < n, gates[h * nb:(h + 1) * nb], MASK_VALUE)
            cnt = jnp.zeros((nb, L), jnp.int32)
            for jp in range(nb):
                gj = gate[jp:jp + 1, :]
                beats = jnp.where(gj > gate, 1, jnp.where((gj == gate) & (blk_id > jp), 1, 0))
                cnt = cnt + beats
            bias_s[h] = jnp.where((cnt < MOBA_TOPK) & (blk_id < n), 0.0, MASK_VALUE)
        k_own = kb_s[n]
        for h in range(nh):
            stage_scores(1, h, jnp.where(causal, _dot_nt(k_own, qb_s[h]), MASK_VALUE))
        scores_into(0, 0)
        m, l, alpha = softmax_update(1, [jnp.zeros((1, L), F32)] * nh,
                                     [jnp.full((1, LANES), -jnp.inf, F32)] * len(pieces),
                                     [jnp.zeros((1, LANES), F32)] * len(pieces))
        return tuple(m), tuple(l), tuple(alpha)

    def stage(prev_blk, blk, next_blk, cur, oth, m, l, alpha, acc):
        acc = pv_update(prev_blk, oth, alpha, acc)
        if next_blk is not None:
            scores_into(oth, next_blk)
        m, l, alpha = softmax_update(cur, [bias_s[h, pl.ds(blk, 1), :] for h in range(nh)], m, l)
        return m, l, alpha, acc

    def query_block(n, state):
        m, l, alpha = (list(c) for c in state)
        acc = [jnp.zeros((ATT_HD, LANES), F32)] * len(pieces)

        def double_step(t, carry):
            m, l, alpha, acc = (list(c) for c in carry)
            j = 2 * t
            m, l, alpha, acc = stage(jnp.where(t == 0, n, j - 1), j, j + 1, 0, 1, m, l, alpha, acc)
            m, l, alpha, acc = stage(j, j + 1, jnp.minimum(j + 2, nb - 1), 1, 0, m, l, alpha, acc)
            return tuple(m), tuple(l), tuple(alpha), tuple(acc)

        def odd_tail(carry):
            m, l, alpha, acc = (list(c) for c in carry)
            m, l, alpha, acc = stage(jnp.where(n == 1, n, n - 2), n - 1, None, 0, 1, m, l, alpha, acc)
            return tuple(m), tuple(l), tuple(alpha), tuple(acc)

        carry = lax.fori_loop(0, n // 2, double_step,
                              (tuple(m), tuple(l), tuple(alpha), tuple(acc)))
        odd = (n & 1) == 1
        m, l, alpha, acc = lax.cond(odd, odd_tail, lambda c: c, carry)
        acc = pv_update(jnp.where(n == 0, n, n - 1), jnp.where(odd, 0, 1), alpha, acc)
        out_t = jnp.concatenate(
            [jnp.concatenate([acc[h * npc + c] / l[h * npc + c] for c in range(npc)], axis=1)
             for h in range(nh)], axis=0)
        oc_ref[q_rows(n), :] = out_t.T.astype(oc_ref.dtype)
        return prepare(jnp.minimum(n + 1, nb - 1))

    lax.fori_loop(0, nb, query_block, prepare(0))


def _moba_prompt(p3, layer, depth, kt_all=None, vt_all=None):
    B, S, _ = p3.shape
    nb = S // MOBA_BLOCK
    npair = ATT_HEADS // HEADS_PER_LANE_TILE
    col = lambda off: pl.BlockSpec((None, S, LANES), lambda b, g, o=off // LANES: (b, 0, o + g))
    kv_out = pl.BlockSpec((None, None, LANES, S), lambda b, g: (b, layer, g, 0))
    carried = [] if kt_all is None else [kt_all, vt_all]
    return pl.pallas_call(
        functools.partial(_moba_prompt_body, nb, len(carried)),
        grid=(B, npair),
        in_specs=[col(OFF_QC), col(OFF_KC), col(OFF_VC)]
        + [pl.BlockSpec(memory_space=pl.ANY)] * len(carried),
        out_specs=[pl.BlockSpec((None, S, LANES), lambda b, g: (b, 0, g)), kv_out, kv_out],
        out_shape=[jax.ShapeDtypeStruct((B, S, ATT_WIDTH), BF16),
                   jax.ShapeDtypeStruct((B, depth, ATT_WIDTH, S), F32),
                   jax.ShapeDtypeStruct((B, depth, ATT_WIDTH, S), F32)],
        input_output_aliases={3: 1, 4: 2} if carried else {},
        scratch_shapes=[pltpu.VMEM((nb, MOBA_BLOCK, LANES), BF16),
                        pltpu.VMEM((nb, LANES, MOBA_BLOCK), BF16),
                        pltpu.VMEM((nb, LANES), F32),
                        pltpu.VMEM((2 * HEADS_PER_LANE_TILE * nb, LANES), BF16),
                        pltpu.VMEM((HEADS_PER_LANE_TILE, nb, MOBA_BLOCK), F32),
                        pltpu.VMEM((HEADS_PER_LANE_TILE, MOBA_BLOCK, LANES), BF16),
                        pltpu.VMEM((2, HEADS_PER_LANE_TILE, MOBA_BLOCK // LANES, MOBA_BLOCK, LANES), F32),
                        pltpu.VMEM((2, HEADS_PER_LANE_TILE, MOBA_BLOCK // LANES, MOBA_BLOCK, LANES), BF16)],
        compiler_params=_params(2),
        name="moba_prompt",
    )(p3, p3, p3, *carried)


PAGES_PER_STEP = 16


def _moba_select_body(n_blocks, pages_per_block, pt_ref, q_ref, *refs):
    pages = refs[:PAGES_PER_STEP]
    idx_ref = refs[PAGES_PER_STEP]
    qcol_s, gate_s = refs[PAGES_PER_STEP + 1:]
    g = pl.program_id(1)
    blocks_per_step = PAGES_PER_STEP // pages_per_block

    @pl.when(g == 0)
    def _():
        q = q_ref[...]
        for c in range(ATT_WIDTH // LANES):
            qcol_s[c * LANES:(c + 1) * LANES, :] = _as_column(q[:, c * LANES:(c + 1) * LANES])
        gate_s[...] = jnp.zeros_like(gate_s)

    q_col = qcol_s[...].reshape(ATT_HEADS, ATT_HD, LANES)
    lane = lax.broadcasted_iota(jnp.int32, gate_s.shape, 1)
    acc = gate_s[...]
    for blk in range(blocks_per_step):
        rows = jnp.sum(pages[blk * pages_per_block][...] * q_col, axis=1)
        for i in range(1, pages_per_block):
            rows = rows + jnp.sum(pages[blk * pages_per_block + i][...] * q_col, axis=1)
        col = jnp.sum(rows, axis=1, keepdims=True)
        acc = jnp.where(lane == g * blocks_per_step + blk, col, acc)
    gate_s[...] = acc

    @pl.when(g == pl.num_programs(1) - 1)
    def _():
        gate = acc * (1.0 / MOBA_BLOCK)
        ln = lax.broadcasted_iota(jnp.int32, gate.shape, 1)
        lnf = ln.astype(F32)
        gate = jnp.where(ln < n_blocks, gate, MASK_VALUE)
        out = jnp.zeros(gate.shape, F32)
        for r in range(MOBA_TOPK):
            top = jnp.max(gate, axis=1, keepdims=True)
            ix = jnp.min(jnp.where(gate == top, lnf, float(LANES)), axis=1, keepdims=True)
            out = jnp.where(ln == r, ix, out)
            gate = jnp.where(lnf == ix, -jnp.inf, gate)
        idx_ref[...] = out.astype(jnp.int32)


def _moba_select(ck_t, page_table, q3, layer):
    NS, n_pages = page_table.shape
    page = ck_t.shape[-1]
    pages_per_block = MOBA_BLOCK // page
    n_blocks = n_pages // pages_per_block
    assert n_blocks <= LANES and n_pages % PAGES_PER_STEP == 0 and page == LANES
    page_spec = lambda i: pl.BlockSpec(
        (None, None, ATT_HEADS, ATT_HD, page),
        lambda b, g, pt, i=i: (pt[b, g * PAGES_PER_STEP + i], layer, 0, 0, 0))
    grid_spec = pltpu.PrefetchScalarGridSpec(
        num_scalar_prefetch=1,
        grid=(NS, n_pages // PAGES_PER_STEP),
        in_specs=[pl.BlockSpec((None, 1, ATT_WIDTH), lambda b, g, pt: (b, 0, 0))]
        + [page_spec(i) for i in range(PAGES_PER_STEP)],
        out_specs=pl.BlockSpec((None, ATT_HEADS, LANES), lambda b, g, pt: (b, 0, 0)),
        scratch_shapes=[pltpu.VMEM((ATT_WIDTH, LANES), F32), pltpu.VMEM((ATT_HEADS, LANES), F32)])
    return pl.pallas_call(
        functools.partial(_moba_select_body, n_blocks, pages_per_block),
        grid_spec=grid_spec,
        out_shape=jax.ShapeDtypeStruct((NS, ATT_HEADS, LANES), jnp.int32),
        compiler_params=_params(2),
        name="moba_select",
    )(page_table, q3, *([ck_t] * PAGES_PER_STEP))


ATTEND_HEADS_PER_STEP = 4


def _moba_sample_attend_body(n_tiles, pt_ref, ix_ref, q_ref, kn_ref, vn_ref, *refs):
    hps = ATTEND_HEADS_PER_STEP
    o_ref = refs[2 * hps * n_tiles]
    scale = ATT_HD ** -0.5
    for hh in range(hps):
        k_tiles = refs[hh * n_tiles:(hh + 1) * n_tiles]
        v_tiles = refs[(hps + hh) * n_tiles:(hps + hh + 1) * n_tiles]
        q = q_ref[hh]
        qb = jnp.broadcast_to(q * scale, (8, ATT_HD)).astype(BF16)
        scores = [_dot(qb, kt[...].astype(BF16))[0:1] for kt in k_tiles]
        s_own = jnp.sum(q * kn_ref[hh], axis=1, keepdims=True) * scale
        m = s_own
        for s in scores:
            m = jnp.maximum(m, jnp.max(s, axis=1, keepdims=True))
        p_own = jnp.exp(s_own - m)
        l = p_own
        o = p_own * vn_ref[hh]
        for s, vt in zip(scores, v_tiles):
            p = jnp.exp(s - m)
            l = l + jnp.sum(p, axis=1, keepdims=True)
            pb = jnp.broadcast_to(p, (8, p.shape[1])).astype(BF16)
            o = o + _dot_nt(pb, vt[...].astype(BF16))[0:1]
        o_ref[hh] = (o / l).astype(o_ref.dtype)


def _moba_sample_attend(ck_t, cv_t, page_table, idx_flat, q4, kn4, vn4, layer):
    NS = page_table.shape[0]
    page = ck_t.shape[-1]
    pages_per_block = MOBA_BLOCK // page
    n_tiles = MOBA_TOPK * pages_per_block
    hps = ATTEND_HEADS_PER_STEP

    def tile_spec(hh, i):
        r, pg = divmod(i, pages_per_block)

        def index(b, hb, pt, ix):
            h = hb * hps + hh
            blk = ix[(b * ATT_HEADS + h) * MOBA_TOPK + r]
            return pt[b, blk * pages_per_block + pg], layer, h, 0, 0

        return pl.BlockSpec((None, None, None, ATT_HD, page), index)

    tiles = [tile_spec(hh, i) for hh in range(hps) for i in range(n_tiles)]
    vec = pl.BlockSpec((None, hps, 1, ATT_HD), lambda b, hb, pt, ix: (b, hb, 0, 0))
    grid_spec = pltpu.PrefetchScalarGridSpec(
        num_scalar_prefetch=2,
        grid=(NS, ATT_HEADS // hps),
        in_specs=[vec, vec, vec] + tiles * 2,
        out_specs=vec)
    return pl.pallas_call(
        functools.partial(_moba_sample_attend_body, n_tiles),
        grid_spec=grid_spec,
        out_shape=jax.ShapeDtypeStruct((NS, ATT_HEADS, 1, ATT_HD), BF16),
        compiler_params=_params(2),
        name="moba_sample_attend",
    )(page_table, idx_flat, q4, kn4, vn4, *([ck_t] * len(tiles)), *([cv_t] * len(tiles)))


def _merge_body(oa_ref, ob_ref, oc_ref, ga_ref, gb_ref, gc_ref, ba_ref, bb_ref, bc_ref,
                x_ref, wa_ref, wb_ref, wc_ref, wo_ref, out_ref):
    mix = (jax.nn.sigmoid(ga_ref[...] + ba_ref[...]) * _dot(oa_ref[...], wa_ref[...])
           + jax.nn.sigmoid(gb_ref[...] + bb_ref[...]) * _dot(ob_ref[...], wb_ref[...])
           + jax.nn.sigmoid(gc_ref[...] + bc_ref[...]) * _dot(oc_ref[...], wc_ref[...]))
    out_ref[...] = x_ref[...] + _dot(mix.astype(BF16), wo_ref[...])


def _merge(oa, ob, oc, p, b_gate, x, wa, wb, wc, wo, tm):
    T, D = x.shape
    W = oa.shape[1]
    br = pl.BlockSpec((tm, W), lambda i: (i, 0))
    gate = lambda k: pl.BlockSpec((tm, D), lambda i, k=k: (i, OFF_GATE // D + k))
    bias = lambda k: pl.BlockSpec((1, D), lambda i, k=k: (0, k))
    wbr = pl.BlockSpec((W, D), lambda i: (0, 0))
    return pl.pallas_call(
        _merge_body,
        grid=(T // tm,),
        in_specs=[br, br, br, gate(0), gate(1), gate(2), bias(0), bias(1), bias(2),
                  pl.BlockSpec((tm, D), lambda i: (i, 0)), wbr, wbr, wbr,
                  pl.BlockSpec((D, D), lambda i: (0, 0))],
        out_specs=pl.BlockSpec((tm, D), lambda i: (i, 0)),
        out_shape=jax.ShapeDtypeStruct((T, D), F32),
        compiler_params=_params(1),
        name="merge",
    )(oa, ob, oc, p, p, p, b_gate, b_gate, b_gate, x, wa, wb, wc, wo)


def _ffn_body(final, tf, x_ref, n2_ref, wg_ref, wu_ref, wd_ref, fn_ref, out_ref):
    x = x_ref[...]
    h = (_rms(x) * n2_ref[...]).astype(BF16)
    acc = None
    for c in range(wd_ref.shape[0] // tf):
        cols = slice(c * tf, (c + 1) * tf)
        act = _silu(_dot(h, wg_ref[:, cols])) * _dot(h, wu_ref[:, cols])
        part = _dot(act.astype(BF16), wd_ref[cols, :])
        acc = part if acc is None else acc + part
    y = x + acc
    if final:
        y = _rms(y) * fn_ref[...]
    out_ref[...] = y


def _ffn(x, n2, w_up, w_down, final_norm, final, tm, tf):
    T, D = x.shape
    d_ff = w_down.shape[0]
    resident = lambda shape, j: pl.BlockSpec(shape, lambda i: (0, j), pipeline_mode=pl.Buffered(1))
    return pl.pallas_call(
        functools.partial(_ffn_body, final, tf),
        grid=(T // tm,),
        in_specs=[pl.BlockSpec((tm, D), lambda i: (i, 0)),
                  pl.BlockSpec((1, D), lambda i: (0, 0)),
                  resident((D, d_ff), 0), resident((D, d_ff), 1), resident((d_ff, D), 0),
                  pl.BlockSpec((1, D), lambda i: (0, 0))],
        out_specs=pl.BlockSpec((tm, D), lambda i: (i, 0)),
        out_shape=jax.ShapeDtypeStruct((T, D), F32),
        compiler_params=_params(1),
        name="ffn",
    )(x, n2, w_up, w_up, w_down, final_norm)


def _tiles(T):
    big = T >= 1024
    return dict(inproj_tm=1024 if big else T, inproj_tn=2048 if big else 1024,
                merge_tm=512 if big else T,
                ffn_tm=512 if big else T, ffn_tf=1408,
                mixer_tc=1024)


def kernel(x_prompt, x_sample, cache_k, cache_v, state_hgrn, state_conv, page_table, norm1, norm2, final_norm, w_in, b_gate, hg_lb_raw, hg_norm, conv_w, w_branch_a, w_branch_b, w_branch_c, w_out, w_ffn_up, w_ffn_down):
    B, S, D = x_prompt.shape
    NS = x_sample.shape[0]
    depth = w_in.shape[0]
    Tp = B * S
    tp, ts = _tiles(Tp), _tiles(NS)

    w_in_b = w_in.astype(BF16)
    wa_b, wb_b, wc_b = (w.astype(BF16) for w in (w_branch_a, w_branch_b, w_branch_c))
    wo_b, wu_b, wd_b = (w.astype(BF16) for w in (w_out, w_ffn_up, w_ffn_down))
    fnorm = final_norm[None]
    ck_t = jnp.transpose(cache_k, (0, 1, 3, 4, 2))
    cv_t = jnp.transpose(cache_v, (0, 1, 3, 4, 2))

    xp = x_prompt.reshape(Tp, D)
    xs = x_sample.reshape(NS, D)
    sp_l, cp_l, ks_l, vs_l, ss_l, cs_l = ([] for _ in range(6))
    kt_all = vt_all = None
    for l in range(depth):
        last = l == depth - 1
        n1, n2, bg = norm1[l][None], norm2[l][None], b_gate[l][None]
        gain = hg_norm[l][None]

        p = _inproj(xp, n1, w_in_b[l], tp["inproj_tm"], tp["inproj_tn"])
        p3 = p.reshape(B, S, -1)
        oa, st = _hgrn_prompt(p3, hg_lb_raw, gain, l, tp["mixer_tc"])
        ob, cbuf = _conv_prompt(p3, conv_w[l], tp["mixer_tc"])
        oc, kt_all, vt_all = _moba_prompt(p3, l, depth, kt_all, vt_all)
        x1 = _merge(oa.reshape(Tp, -1), ob.reshape(Tp, -1), oc.reshape(Tp, -1), p, bg, xp,
                    wa_b[l], wb_b[l], wc_b[l], wo_b[l], tp["merge_tm"])
        xp = _ffn(x1, n2, wu_b[l], wd_b[l], fnorm, last, tp["ffn_tm"], tp["ffn_tf"])
        sp_l.append(st)
        cp_l.append(cbuf)

        ps = _inproj(xs, n1, w_in_b[l], ts["inproj_tm"], ts["inproj_tn"])
        ps3 = ps.reshape(NS, 1, -1)
        oa_s, st_s = _hgrn_sample(ps3, state_hgrn, hg_lb_raw, gain, l)
        ob_s, z_s = _conv_sample(ps, conv_w[l], state_conv[:, l, 0], state_conv[:, l, 1])
        q_s = ps[:, OFF_QC:OFF_QC + ATT_WIDTH]
        k_s = ps[:, OFF_KC:OFF_KC + ATT_WIDTH]
        v_s = ps[:, OFF_VC:OFF_VC + ATT_WIDTH]
        sel = _moba_select(ck_t, page_table, q_s.reshape(NS, 1, ATT_WIDTH), l)
        idx = sel[:, :, :MOBA_TOPK].reshape(-1)
        to4 = lambda a: a.reshape(NS, ATT_HEADS, 1, ATT_HD)
        oc_s = _moba_sample_attend(ck_t, cv_t, page_table, idx, to4(q_s), to4(k_s), to4(v_s), l)
        x1s = _merge(oa_s.reshape(NS, -1), ob_s, oc_s.reshape(NS, -1), ps, bg, xs,
                     wa_b[l], wb_b[l], wc_b[l], wo_b[l], ts["merge_tm"])
        xs = _ffn(x1s, n2, wu_b[l], wd_b[l], fnorm, last, ts["ffn_tm"], ts["ffn_tf"])
        ks_l.append(k_s.reshape(NS, 1, ATT_HEADS, ATT_HD))
        vs_l.append(v_s.reshape(NS, 1, ATT_HEADS, ATT_HD))
        ss_l.append(st_s)
        cs_l.append(jnp.stack([state_conv[:, l, 1], z_s], axis=1))

    from_t = lambda a: jnp.transpose(a.reshape(B, depth, ATT_HEADS, ATT_HD, S), (0, 1, 4, 2, 3))
    return (xp.reshape(B, S, D), xs.reshape(NS, 1, D),
            from_t(kt_all), from_t(vt_all),
            jnp.stack(sp_l, axis=1), jnp.stack(cp_l, axis=1),
            jnp.stack(ks_l, axis=1), jnp.stack(vs_l, axis=1),
            jnp.stack(ss_l, axis=1), jnp.stack(cs_l, axis=1))
```

```python
import functools

import jax
import jax.numpy as jnp
from jax import lax
from jax.experimental import pallas as pl
from jax.experimental.pallas import tpu as pltpu

F32 = jnp.float32
BF16 = jnp.bfloat16

EPS = 1e-6
MASK_VALUE = -1e30
EXP_CLAMP = 60.0
HG_HEADS = 4
HG_DK = 128
HG_DV = 128
HG_WIDTH = HG_HEADS * HG_DK
HG_CHUNK = 32
CONV_DIM = 512
CONV_W = 3
ATT_HEADS = 8
ATT_HD = 64
ATT_WIDTH = ATT_HEADS * ATT_HD
MOBA_BLOCK = 256
MOBA_TOPK = 3
N_BRANCH = 3
LANES = 128
HEADS_PER_LANE_TILE = LANES // ATT_HD

OFF_QA, OFF_FA, OFF_IA, OFF_OGA = 0, 512, 1024, 1536
OFF_BG, OFF_CG, OFF_HC = 2048, 2560, 3072
OFF_QC, OFF_KC, OFF_VC = 3584, 4096, 4608
OFF_GATE = 5120

VMEM_LIMIT = 56 * 1024 * 1024
HIGHEST = lax.Precision.HIGHEST


def _params(n_axes):
    return pltpu.CompilerParams(dimension_semantics=("arbitrary",) * n_axes,
                                vmem_limit_bytes=VMEM_LIMIT)


def _dot(a, b):
    return jnp.dot(a, b, preferred_element_type=F32)


def _dot_nt(a, b, precision=None):
    return lax.dot_general(a, b, (((1,), (1,)), ((), ())),
                           preferred_element_type=F32, precision=precision)


def _dot_tn(a, b):
    return lax.dot_general(a, b, (((0,), (0,)), ((), ())), preferred_element_type=F32)


def _rms(x):
    return x * lax.rsqrt(jnp.mean(x * x, axis=-1, keepdims=True) + EPS)


def _silu(x):
    h = 0.5 * x
    return h + h * jnp.tanh(h)


def _log_sigmoid(a):
    return jnp.minimum(a, 0.0) - jnp.log1p(jnp.exp(-jnp.abs(a)))


def _hgrn_gate_inputs(qa, fa, lb):
    q = _silu(qa)
    logf = _log_sigmoid(fa) + jnp.log1p(lb * jnp.exp(jnp.minimum(-fa, EXP_CLAMP)))
    t = jnp.tanh(0.5 * logf)
    k = -2.0 * t / (1.0 - t)
    return q, k, logf


def _layer_lower_bound(raw, layer):
    e = jnp.exp(raw - jnp.max(raw, axis=0, keepdims=True))
    pr = e / jnp.sum(e, axis=0, keepdims=True)
    lb = jnp.zeros_like(pr[0:1])
    for i in range(1, layer + 1):
        lb = lb + pr[i:i + 1]
    return lb


def _inproj_body(x_ref, g_ref, w_ref, o_ref, h_scr):
    @pl.when(pl.program_id(1) == 0)
    def _():
        h_scr[...] = (_rms(x_ref[...]) * g_ref[...]).astype(BF16)

    o_ref[...] = _dot(h_scr[...], w_ref[...])


def _inproj(x2d, gain, w_bf, layer, tm, tn):
    T, D = x2d.shape
    N = w_bf.shape[2]
    return pl.pallas_call(
        _inproj_body,
        grid=(T // tm, N // tn),
        in_specs=[pl.BlockSpec((tm, D), lambda i, j: (i, 0)),
                  pl.BlockSpec((1, D), lambda i, j: (0, 0)),
                  pl.BlockSpec((None, D, tn), lambda i, j: (layer, 0, j))],
        out_specs=pl.BlockSpec((tm, tn), lambda i, j: (i, j)),
        out_shape=jax.ShapeDtypeStruct((T, N), F32),
        scratch_shapes=[pltpu.VMEM((tm, D), BF16)],
        compiler_params=_params(2),
        name="inproj",
    )(x2d, gain, w_bf)


def _chunk_cumsum(x, chunk):
    rin = lax.broadcasted_iota(jnp.int32, x.shape, 0) % chunk
    shift = 1
    while shift < chunk:
        x = x + jnp.where(rin >= shift, pltpu.roll(x, shift, 0), 0.0)
        shift *= 2
    return x


def _hgrn_prompt_body(layer, nchunk, q_ref, f_ref, i_ref, og_ref, lbr_ref, gn_ref,
                      oa_ref, st_ref, st_s):
    C = HG_CHUNK
    t = pl.program_id(2)
    tc = q_ref.shape[0]

    @pl.when(t == 0)
    def _():
        st_s[...] = jnp.zeros_like(st_s)

    lb = _layer_lower_bound(lbr_ref[...], layer)
    q, k, logf = _hgrn_gate_inputs(q_ref[...], f_ref[...], lb)
    b = _chunk_cumsum(logf, C)

    def per_chunk_row(r):
        rows = b.reshape(nchunk, C, HG_DK)[:, r:r + 1, :]
        return jnp.broadcast_to(rows, (nchunk, C, HG_DK)).reshape(tc, HG_DK)

    b_ref = per_chunk_row(C // 2)
    b_last = per_chunk_row(C - 1)
    qx = (q * jnp.exp(b - b_ref)).astype(BF16)
    kx = (k * jnp.exp(b_ref - b)).astype(BF16)
    qd = (q * jnp.exp(b)).astype(BF16)
    kd = (k * jnp.exp(b_last - b)).astype(BF16)
    decay = jnp.exp(b_last)
    vb = i_ref[...].astype(BF16)

    row = lax.broadcasted_iota(jnp.int32, (C, C), 0)
    col = lax.broadcasted_iota(jnp.int32, (C, C), 1)
    causal = row >= col

    st = st_s[...]
    outs = []
    for c in range(nchunk):
        rows = slice(c * C, (c + 1) * C)
        a = jnp.where(causal, _dot_nt(qx[rows], kx[rows]), 0.0)
        o = _dot(a.astype(BF16), vb[rows]) + _dot_nt(qd[rows], st.astype(BF16))
        st = st * decay[c * C:c * C + 1] + _dot_tn(vb[rows], kd[rows])
        outs.append(o)
    st_s[...] = st

    o = _rms(jnp.concatenate(outs, axis=0)) * gn_ref[...]
    oa_ref[...] = (o * _silu(og_ref[...])).astype(oa_ref.dtype)

    @pl.when(t == pl.num_programs(2) - 1)
    def _():
        st_ref[...] = st.T


def _hgrn_prompt(p3, lb_raw, gain, layer, tc):
    B, S, _ = p3.shape
    depth = lb_raw.shape[0]
    blk = lambda off: pl.BlockSpec((None, tc, HG_DK),
                                   lambda b, h, t, o=off // HG_DK: (b, t, o + h))
    return pl.pallas_call(
        functools.partial(_hgrn_prompt_body, layer, tc // HG_CHUNK),
        grid=(B, HG_HEADS, S // tc),
        in_specs=[blk(OFF_QA), blk(OFF_FA), blk(OFF_IA), blk(OFF_OGA),
                  pl.BlockSpec((depth, HG_DK), lambda b, h, t: (0, h)),
                  pl.BlockSpec((1, HG_DV), lambda b, h, t: (0, 0))],
        out_specs=[pl.BlockSpec((None, tc, HG_DV), lambda b, h, t: (b, t, h)),
                   pl.BlockSpec((None, None, HG_DK, HG_DV), lambda b, h, t: (b, h, 0, 0))],
        out_shape=[jax.ShapeDtypeStruct((B, S, HG_HEADS * HG_DV), BF16),
                   jax.ShapeDtypeStruct((B, HG_HEADS, HG_DK, HG_DV), F32)],
        scratch_shapes=[pltpu.VMEM((HG_DV, HG_DK), F32)],
        compiler_params=_params(3),
        name="hgrn_prompt",
    )(p3, p3, p3, p3, lb_raw, gain)


def _as_column(row):
    n = row.shape[-1]
    return jnp.broadcast_to(row, (n, n)).T


def _hgrn_sample_body(layer, q_ref, f_ref, i_ref, og_ref, lbr_ref, gn_ref, s0_ref,
                      oa_ref, st_ref):
    lb_all = _layer_lower_bound(lbr_ref[...], layer)
    q_all, k_all, logf_all = _hgrn_gate_inputs(q_ref[...], f_ref[...], lb_all)
    v_all, og_all = i_ref[...], og_ref[...]
    for h in range(HG_HEADS):
        hs = slice(h * HG_DK, (h + 1) * HG_DK)
        s = _as_column(jnp.exp(logf_all[:, hs])) * s0_ref[h] + _as_column(k_all[:, hs]) * v_all[:, hs]
        st_ref[h] = s
        o = jnp.sum(_as_column(q_all[:, hs]) * s, axis=0, keepdims=True)
        o = _rms(o) * gn_ref[...]
        oa_ref[:, hs] = (o * _silu(og_all[:, hs])).astype(oa_ref.dtype)


def _hgrn_sample(ps3, state, lb_raw, gain, layer):
    NS = ps3.shape[0]
    depth = lb_raw.shape[0]
    blk = lambda off: pl.BlockSpec((None, 1, HG_WIDTH), lambda b, o=off // HG_WIDTH: (b, 0, o))
    return pl.pallas_call(
        functools.partial(_hgrn_sample_body, layer),
        grid=(NS,),
        in_specs=[blk(OFF_QA), blk(OFF_FA), blk(OFF_IA), blk(OFF_OGA),
                  pl.BlockSpec((depth, HG_WIDTH), lambda b: (0, 0)),
                  pl.BlockSpec((1, HG_DV), lambda b: (0, 0)),
                  pl.BlockSpec((None, None, HG_HEADS, HG_DK, HG_DV),
                               lambda b: (b, layer, 0, 0, 0))],
        out_specs=[pl.BlockSpec((None, 1, HG_HEADS * HG_DV), lambda b: (b, 0, 0)),
                   pl.BlockSpec((None, HG_HEADS, HG_DK, HG_DV), lambda b: (b, 0, 0, 0))],
        out_shape=[jax.ShapeDtypeStruct((NS, 1, HG_HEADS * HG_DV), BF16),
                   jax.ShapeDtypeStruct((NS, HG_HEADS, HG_DK, HG_DV), F32)],
        compiler_params=_params(1),
        name="hgrn_sample",
    )(ps3, ps3, ps3, ps3, lb_raw, gain, state)


CONV_PAD = 8


def _conv_prompt_body(bg_ref, cg_ref, hc_ref, w_ref, ob_ref, buf_ref, z_s):
    t = pl.program_id(1)
    tc = bg_ref.shape[0]

    @pl.when(t == 0)
    def _():
        z_s[0:CONV_PAD, :] = jnp.zeros((CONV_PAD, CONV_DIM), F32)

    z = cg_ref[...] * hc_ref[...]
    z_s[CONV_PAD:CONV_PAD + tc, :] = z
    w = w_ref[...]
    y = w[CONV_W - 1:CONV_W] * z
    for i in range(CONV_W - 1):
        shift = CONV_W - 1 - i
        y = y + w[i:i + 1] * z_s[CONV_PAD - shift:CONV_PAD - shift + tc, :]
    ob_ref[...] = (bg_ref[...] * y).astype(ob_ref.dtype)
    z_s[0:CONV_PAD, :] = z[tc - CONV_PAD:tc, :]

    @pl.when(t == pl.num_programs(1) - 1)
    def _():
        buf_ref[...] = z[tc - (CONV_W - 1):tc, :]


def _conv_prompt(p3, w, tc):
    B, S, _ = p3.shape
    blk = lambda off: pl.BlockSpec((None, tc, CONV_DIM),
                                   lambda b, t, o=off // CONV_DIM: (b, t, o))
    return pl.pallas_call(
        _conv_prompt_body,
        grid=(B, S // tc),
        in_specs=[blk(OFF_BG), blk(OFF_CG), blk(OFF_HC),
                  pl.BlockSpec((CONV_W, CONV_DIM), lambda b, t: (0, 0))],
        out_specs=[pl.BlockSpec((None, tc, CONV_DIM), lambda b, t: (b, t, 0)),
                   pl.BlockSpec((None, CONV_W - 1, CONV_DIM), lambda b, t: (b, 0, 0))],
        out_shape=[jax.ShapeDtypeStruct((B, S, CONV_DIM), BF16),
                   jax.ShapeDtypeStruct((B, CONV_W - 1, CONV_DIM), F32)],
        scratch_shapes=[pltpu.VMEM((tc + CONV_PAD, CONV_DIM), F32)],
        compiler_params=_params(2),
        name="conv_prompt",
    )(p3, p3, p3, w)


def _conv_sample_body(bg_ref, cg_ref, hc_ref, w_ref, b0_ref, b1_ref, ob_ref, z_ref):
    z = cg_ref[...] * hc_ref[...]
    w = w_ref[...]
    y = w[0:1] * b0_ref[...] + w[1:2] * b1_ref[...] + w[2:3] * z
    ob_ref[...] = (bg_ref[...] * y).astype(ob_ref.dtype)
    z_ref[...] = z


def _conv_sample(ps, w, buf0, buf1):
    NS = ps.shape[0]
    blk = lambda off: pl.BlockSpec((NS, CONV_DIM), lambda i, o=off // CONV_DIM: (0, o))
    full = pl.BlockSpec((NS, CONV_DIM), lambda i: (0, 0))
    return pl.pallas_call(
        _conv_sample_body,
        grid=(1,),
        in_specs=[blk(OFF_BG), blk(OFF_CG), blk(OFF_HC),
                  pl.BlockSpec((CONV_W, CONV_DIM), lambda i: (0, 0)), full, full],
        out_specs=[full, full],
        out_shape=[jax.ShapeDtypeStruct((NS, CONV_DIM), BF16),
                   jax.ShapeDtypeStruct((NS, CONV_DIM), F32)],
        compiler_params=_params(1),
        name="conv_sample",
    )(ps, ps, ps, w, buf0, buf1)


def _moba_prompt_body(nb, n_carried, q_ref, k_ref, v_ref, *refs):
    (oc_ref, kt_ref, vt_ref,
     kb_s, vt_s, mean_s, msplit_s, bias_s, qb_s, s_buf, p_buf) = refs[n_carried:]
    L = MOBA_BLOCK
    nh = HEADS_PER_LANE_TILE

    for blk in range(nb):
        rows = slice(blk * L, (blk + 1) * L)
        kblk = k_ref[rows, :]
        kb_s[blk] = kblk.astype(BF16)
        mean_s[blk:blk + 1, :] = jnp.mean(kblk, axis=0, keepdims=True)
        kt_ref[:, rows] = kblk.T
        vt = v_ref[rows, :].T
        vt_ref[:, rows] = vt
        vt_s[blk] = vt.astype(BF16)
    mean = mean_s[...]
    mean_head = lax.broadcasted_iota(jnp.int32, mean.shape, 1) // ATT_HD
    for h in range(nh):
        mh = jnp.where(mean_head == h, mean, 0.0)
        hi = mh.astype(BF16)
        msplit_s[h * nb:(h + 1) * nb, :] = hi
        msplit_s[(nh + h) * nb:(nh + h + 1) * nb, :] = (mh - hi.astype(F32)).astype(BF16)

    npc = L // LANES
    pieces = [(h, c) for h in range(nh) for c in range(npc)]
    blk_id = lax.broadcasted_iota(jnp.int32, (nb, L), 0)
    lane_head = lax.broadcasted_iota(jnp.int32, (L, LANES), 1) // ATT_HD
    causal = (lax.broadcasted_iota(jnp.int32, (L, L), 0)
              <= lax.broadcasted_iota(jnp.int32, (L, L), 1))

    def q_rows(n):
        return pl.ds(pl.multiple_of(n * L, L), L)

    def pv_update(blk, slot, alpha, acc):
        vt = vt_s[blk]
        out = []
        for h in range(nh):
            p = jnp.concatenate([p_buf[slot, h, c] for c in range(npc)], axis=1)
            pv = _dot(vt[h * ATT_HD:(h + 1) * ATT_HD], p)
            out += [alpha[h * npc + c] * acc[h * npc + c] + pv[:, c * LANES:(c + 1) * LANES]
                    for c in range(npc)]
        return out

    def softmax_update(slot, bias_rows, m, l):
        m_out, l_out, alpha = [], [], []
        for i, (h, c) in enumerate(pieces):
            cols = slice(c * LANES, (c + 1) * LANES)
            s = s_buf[slot, h, c]
            b = bias_rows[h][:, cols]
            m_new = jnp.maximum(m[i], jnp.max(s, axis=0, keepdims=True) + b)
            a = jnp.exp(m[i] - m_new)
            p = jnp.exp(s - (m_new - b))
            p_buf[slot, h, c] = p.astype(BF16)
            m_out.append(m_new)
            l_out.append(a * l[i] + jnp.sum(p, axis=0, keepdims=True))
            alpha.append(a)
        return m_out, l_out, alpha

    def stage_scores(slot, h, s):
        for c in range(npc):
            s_buf[slot, h, c] = s[:, c * LANES:(c + 1) * LANES]

    def scores_into(slot, blk):
        kblk = kb_s[blk]
        for h in range(nh):
            stage_scores(slot, h, _dot_nt(kblk, qb_s[h]))

    def prepare(n):
        qf = q_ref[q_rows(n), :]
        q_hi = qf.astype(BF16)
        q_lo = (qf - q_hi.astype(F32)).astype(BF16)
        g_hi = _dot_nt(msplit_s[...], q_hi)
        gates = g_hi[:nh * nb] + g_hi[nh * nb:] + _dot_nt(msplit_s[:nh * nb, :], q_lo)
        for h in range(nh):
            qh = jnp.where(lane_head == h, qf, 0.0)
            qb_s[h] = (qh * (ATT_HD ** -0.5)).astype(BF16)
            gate = jnp.where(blk_id < n, gates[h * nb:(h + 1) * nb], MASK_VALUE)
            cnt = jnp.zeros((nb, L), jnp.int32)
            for jp in range(nb):
                gj = gate[jp:jp + 1, :]
                beats = jnp.where(gj > gate, 1, jnp.where((gj == gate) & (blk_id > jp), 1, 0))
                cnt = cnt + beats
            bias_s[h] = jnp.where((cnt < MOBA_TOPK) & (blk_id < n), 0.0, MASK_VALUE)
        k_own = kb_s[n]
        for h in range(nh):
            stage_scores(1, h, jnp.where(causal, _dot_nt(k_own, qb_s[h]), MASK_VALUE))
        scores_into(0, 0)
        m, l, alpha = softmax_update(1, [jnp.zeros((1, L), F32)] * nh,
                                     [jnp.full((1, LANES), -jnp.inf, F32)] * len(pieces),
                                     [jnp.zeros((1, LANES), F32)] * len(pieces))
        return tuple(m), tuple(l), tuple(alpha)

    def stage(prev_blk, blk, next_blk, cur, oth, m, l, alpha, acc):
        acc = pv_update(prev_blk, oth, alpha, acc)
        if next_blk is not None:
            scores_into(oth, next_blk)
        m, l, alpha = softmax_update(cur, [bias_s[h, pl.ds(blk, 1), :] for h in range(nh)], m, l)
        return m, l, alpha, acc

    def query_block(n, state):
        m, l, alpha = (list(c) for c in state)
        acc = [jnp.zeros((ATT_HD, LANES), F32)] * len(pieces)

        def double_step(t, carry):
            m, l, alpha, acc = (list(c) for c in carry)
            j = 2 * t
            m, l, alpha, acc = stage(jnp.where(t == 0, n, j - 1), j, j + 1, 0, 1, m, l, alpha, acc)
            m, l, alpha, acc = stage(j, j + 1, jnp.minimum(j + 2, nb - 1), 1, 0, m, l, alpha, acc)
            return tuple(m), tuple(l), tuple(alpha), tuple(acc)

        def odd_tail(carry):
            m, l, alpha, acc = (list(c) for c in carry)
            m, l, alpha, acc = stage(jnp.where(n == 1, n, n - 2), n - 1, None, 0, 1, m, l, alpha, acc)
            return tuple(m), tuple(l), tuple(alpha), tuple(acc)

        carry = lax.fori_loop(0, n // 2, double_step,
                              (tuple(m), tuple(l), tuple(alpha), tuple(acc)))
        odd = (n & 1) == 1
        m, l, alpha, acc = lax.cond(odd, odd_tail, lambda c: c, carry)
        acc = pv_update(jnp.where(n == 0, n, n - 1), jnp.where(odd, 0, 1), alpha, acc)
        out_t = jnp.concatenate(
            [jnp.concatenate([acc[h * npc + c] / l[h * npc + c] for c in range(npc)], axis=1)
             for h in range(nh)], axis=0)
        oc_ref[q_rows(n), :] = out_t.T.astype(oc_ref.dtype)
        return prepare(jnp.minimum(n + 1, nb - 1))

    lax.fori_loop(0, nb, query_block, prepare(0))


def _moba_prompt(p3, layer, depth, kt_all=None, vt_all=None):
    B, S, _ = p3.shape
    nb = S // MOBA_BLOCK
    npair = ATT_HEADS // HEADS_PER_LANE_TILE
    col = lambda off: pl.BlockSpec((None, S, LANES), lambda b, g, o=off // LANES: (b, 0, o + g))
    kv_out = pl.BlockSpec((None, None, LANES, S), lambda b, g: (b, layer, g, 0))
    carried = [] if kt_all is None else [kt_all, vt_all]
    return pl.pallas_call(
        functools.partial(_moba_prompt_body, nb, len(carried)),
        grid=(B, npair),
        in_specs=[col(OFF_QC), col(OFF_KC), col(OFF_VC)]
        + [pl.BlockSpec(memory_space=pl.ANY)] * len(carried),
        out_specs=[pl.BlockSpec((None, S, LANES), lambda b, g: (b, 0, g)), kv_out, kv_out],
        out_shape=[jax.ShapeDtypeStruct((B, S, ATT_WIDTH), BF16),
                   jax.ShapeDtypeStruct((B, depth, ATT_WIDTH, S), F32),
                   jax.ShapeDtypeStruct((B, depth, ATT_WIDTH, S), F32)],
        input_output_aliases={3: 1, 4: 2} if carried else {},
        scratch_shapes=[pltpu.VMEM((nb, MOBA_BLOCK, LANES), BF16),
                        pltpu.VMEM((nb, LANES, MOBA_BLOCK), BF16),
                        pltpu.VMEM((nb, LANES), F32),
                        pltpu.VMEM((2 * HEADS_PER_LANE_TILE * nb, LANES), BF16),
                        pltpu.VMEM((HEADS_PER_LANE_TILE, nb, MOBA_BLOCK), F32),
                        pltpu.VMEM((HEADS_PER_LANE_TILE, MOBA_BLOCK, LANES), BF16),
                        pltpu.VMEM((2, HEADS_PER_LANE_TILE, MOBA_BLOCK // LANES, MOBA_BLOCK, LANES), F32),
                        pltpu.VMEM((2, HEADS_PER_LANE_TILE, MOBA_BLOCK // LANES, MOBA_BLOCK, LANES), BF16)],
        compiler_params=_params(2),
        name="moba_prompt",
    )(p3, p3, p3, *carried)


PAGES_PER_STEP = 16


def _moba_select_body(n_blocks, pages_per_block, pt_ref, q_ref, *refs):
    pages = refs[:PAGES_PER_STEP]
    idx_ref = refs[PAGES_PER_STEP]
    qcol_s, gate_s = refs[PAGES_PER_STEP + 1:]
    g = pl.program_id(1)
    blocks_per_step = PAGES_PER_STEP // pages_per_block

    @pl.when(g == 0)
    def _():
        q = q_ref[...]
        for c in range(ATT_WIDTH // LANES):
            qcol_s[c * LANES:(c + 1) * LANES, :] = _as_column(q[:, c * LANES:(c + 1) * LANES])
        gate_s[...] = jnp.zeros_like(gate_s)

    q_col = qcol_s[...].reshape(ATT_HEADS, ATT_HD, LANES)
    lane = lax.broadcasted_iota(jnp.int32, gate_s.shape, 1)
    acc = gate_s[...]
    for blk in range(blocks_per_step):
        rows = jnp.sum(pages[blk * pages_per_block][...] * q_col, axis=1)
        for i in range(1, pages_per_block):
            rows = rows + jnp.sum(pages[blk * pages_per_block + i][...] * q_col, axis=1)
        col = jnp.sum(rows, axis=1, keepdims=True)
        acc = jnp.where(lane == g * blocks_per_step + blk, col, acc)
    gate_s[...] = acc

    @pl.when(g == pl.num_programs(1) - 1)
    def _():
        gate = acc * (1.0 / MOBA_BLOCK)
        ln = lax.broadcasted_iota(jnp.int32, gate.shape, 1)
        lnf = ln.astype(F32)
        gate = jnp.where(ln < n_blocks, gate, MASK_VALUE)
        out = jnp.zeros(gate.shape, F32)
        for r in range(MOBA_TOPK):
            top = jnp.max(gate, axis=1, keepdims=True)
            ix = jnp.min(jnp.where(gate == top, lnf, float(LANES)), axis=1, keepdims=True)
            out = jnp.where(ln == r, ix, out)
            gate = jnp.where(lnf == ix, -jnp.inf, gate)
        idx_ref[...] = out.astype(jnp.int32)


def _moba_select(ck_t, page_table, q3, layer):
    NS, n_pages = page_table.shape
    page = ck_t.shape[-1]
    pages_per_block = MOBA_BLOCK // page
    n_blocks = n_pages // pages_per_block
    assert n_blocks <= LANES and n_pages % PAGES_PER_STEP == 0 and page == LANES
    page_spec = lambda i: pl.BlockSpec(
        (None, None, ATT_HEADS, ATT_HD, page),
        lambda b, g, pt, i=i: (pt[b, g * PAGES_PER_STEP + i], layer, 0, 0, 0))
    grid_spec = pltpu.PrefetchScalarGridSpec(
        num_scalar_prefetch=1,
        grid=(NS, n_pages // PAGES_PER_STEP),
        in_specs=[pl.BlockSpec((None, 1, ATT_WIDTH), lambda b, g, pt: (b, 0, 0))]
        + [page_spec(i) for i in range(PAGES_PER_STEP)],
        out_specs=pl.BlockSpec((None, ATT_HEADS, LANES), lambda b, g, pt: (b, 0, 0)),
        scratch_shapes=[pltpu.VMEM((ATT_WIDTH, LANES), F32), pltpu.VMEM((ATT_HEADS, LANES), F32)])
    return pl.pallas_call(
        functools.partial(_moba_select_body, n_blocks, pages_per_block),
        grid_spec=grid_spec,
        out_shape=jax.ShapeDtypeStruct((NS, ATT_HEADS, LANES), jnp.int32),
        compiler_params=_params(2),
        name="moba_select",
    )(page_table, q3, *([ck_t] * PAGES_PER_STEP))


ATTEND_HEADS_PER_STEP = 4


def _moba_sample_attend_body(n_tiles, pt_ref, ix_ref, q_ref, kn_ref, vn_ref, *refs):
    hps = ATTEND_HEADS_PER_STEP
    o_ref = refs[2 * hps * n_tiles]
    scale = ATT_HD ** -0.5
    for hh in range(hps):
        k_tiles = refs[hh * n_tiles:(hh + 1) * n_tiles]
        v_tiles = refs[(hps + hh) * n_tiles:(hps + hh + 1) * n_tiles]
        q = q_ref[hh]
        qb = jnp.broadcast_to(q * scale, (8, ATT_HD)).astype(BF16)
        scores = [_dot(qb, kt[...].astype(BF16))[0:1] for kt in k_tiles]
        s_own = jnp.sum(q * kn_ref[hh], axis=1, keepdims=True) * scale
        m = s_own
        for s in scores:
            m = jnp.maximum(m, jnp.max(s, axis=1, keepdims=True))
        p_own = jnp.exp(s_own - m)
        l = p_own
        o = p_own * vn_ref[hh]
        for s, vt in zip(scores, v_tiles):
            p = jnp.exp(s - m)
            l = l + jnp.sum(p, axis=1, keepdims=True)
            pb = jnp.broadcast_to(p, (8, p.shape[1])).astype(BF16)
            o = o + _dot_nt(pb, vt[...].astype(BF16))[0:1]
        o_ref[hh] = (o / l).astype(o_ref.dtype)


def _moba_sample_attend(ck_t, cv_t, page_table, idx_flat, q4, kn4, vn4, layer):
    NS = page_table.shape[0]
    page = ck_t.shape[-1]
    pages_per_block = MOBA_BLOCK // page
    n_tiles = MOBA_TOPK * pages_per_block
    hps = ATTEND_HEADS_PER_STEP

    def tile_spec(hh, i):
        r, pg = divmod(i, pages_per_block)

        def index(b, hb, pt, ix):
            h = hb * hps + hh
            blk = ix[(b * ATT_HEADS + h) * MOBA_TOPK + r]
            return pt[b, blk * pages_per_block + pg], layer, h, 0, 0

        return pl.BlockSpec((None, None, None, ATT_HD, page), index)

    tiles = [tile_spec(hh, i) for hh in range(hps) for i in range(n_tiles)]
    vec = pl.BlockSpec((None, hps, 1, ATT_HD), lambda b, hb, pt, ix: (b, hb, 0, 0))
    grid_spec = pltpu.PrefetchScalarGridSpec(
        num_scalar_prefetch=2,
        grid=(NS, ATT_HEADS // hps),
        in_specs=[vec, vec, vec] + tiles * 2,
        out_specs=vec)
    return pl.pallas_call(
        functools.partial(_moba_sample_attend_body, n_tiles),
        grid_spec=grid_spec,
        out_shape=jax.ShapeDtypeStruct((NS, ATT_HEADS, 1, ATT_HD), BF16),
        compiler_params=_params(2),
        name="moba_sample_attend",
    )(page_table, idx_flat, q4, kn4, vn4, *([ck_t] * len(tiles)), *([cv_t] * len(tiles)))


def _merge_body(oa_ref, ob_ref, oc_ref, ga_ref, gb_ref, gc_ref, ba_ref, bb_ref, bc_ref,
                x_ref, wa_ref, wb_ref, wc_ref, wo_ref, out_ref):
    mix = (jax.nn.sigmoid(ga_ref[...] + ba_ref[...]) * _dot(oa_ref[...], wa_ref[...])
           + jax.nn.sigmoid(gb_ref[...] + bb_ref[...]) * _dot(ob_ref[...], wb_ref[...])
           + jax.nn.sigmoid(gc_ref[...] + bc_ref[...]) * _dot(oc_ref[...], wc_ref[...]))
    out_ref[...] = x_ref[...] + _dot(mix.astype(BF16), wo_ref[...])


def _merge(oa, ob, oc, p, b_gate, x, wa, wb, wc, wo, layer, tm):
    T, D = x.shape
    W = oa.shape[1]
    br = pl.BlockSpec((tm, W), lambda i: (i, 0))
    gate = lambda k: pl.BlockSpec((tm, D), lambda i, k=k: (i, OFF_GATE // D + k))
    bias = lambda k: pl.BlockSpec((1, D), lambda i, k=k: (0, k))
    wbr = pl.BlockSpec((None, W, D), lambda i: (layer, 0, 0))
    return pl.pallas_call(
        _merge_body,
        grid=(T // tm,),
        in_specs=[br, br, br, gate(0), gate(1), gate(2), bias(0), bias(1), bias(2),
                  pl.BlockSpec((tm, D), lambda i: (i, 0)), wbr, wbr, wbr,
                  pl.BlockSpec((None, D, D), lambda i: (layer, 0, 0))],
        out_specs=pl.BlockSpec((tm, D), lambda i: (i, 0)),
        out_shape=jax.ShapeDtypeStruct((T, D), F32),
        compiler_params=_params(1),
        name="merge",
    )(oa, ob, oc, p, p, p, b_gate, b_gate, b_gate, x, wa, wb, wc, wo)


def _ffn_body(final, tf, x_ref, n2_ref, wg_ref, wu_ref, wd_ref, fn_ref, out_ref):
    x = x_ref[...]
    h = (_rms(x) * n2_ref[...]).astype(BF16)
    acc = None
    for c in range(wd_ref.shape[0] // tf):
        cols = slice(c * tf, (c + 1) * tf)
        act = _silu(_dot(h, wg_ref[:, cols])) * _dot(h, wu_ref[:, cols])
        part = _dot(act.astype(BF16), wd_ref[cols, :])
        acc = part if acc is None else acc + part
    y = x + acc
    if final:
        y = _rms(y) * fn_ref[...]
    out_ref[...] = y


def _ffn(x, n2, w_up, w_down, layer, final_norm, final, tm, tf):
    T, D = x.shape
    d_ff = w_down.shape[1]
    resident = lambda shape, j: pl.BlockSpec((None,) + shape, lambda i: (layer, 0, j),
                                             pipeline_mode=pl.Buffered(1))
    return pl.pallas_call(
        functools.partial(_ffn_body, final, tf),
        grid=(T // tm,),
        in_specs=[pl.BlockSpec((tm, D), lambda i: (i, 0)),
                  pl.BlockSpec((1, D), lambda i: (0, 0)),
                  resident((D, d_ff), 0), resident((D, d_ff), 1), resident((d_ff, D), 0),
                  pl.BlockSpec((1, D), lambda i: (0, 0))],
        out_specs=pl.BlockSpec((tm, D), lambda i: (i, 0)),
        out_shape=jax.ShapeDtypeStruct((T, D), F32),
        compiler_params=_params(1),
        name="ffn",
    )(x, n2, w_up, w_up, w_down, final_norm)


def _tiles(T):
    big = T >= 1024
    return dict(inproj_tm=1024 if big else T, inproj_tn=2048 if big else 1024,
                merge_tm=512 if big else T,
                ffn_tm=512 if big else T, ffn_tf=1408,
                mixer_tc=2048)


def kernel(x_prompt, x_sample, cache_k, cache_v, state_hgrn, state_conv, page_table, norm1, norm2, final_norm, w_in, b_gate, hg_lb_raw, hg_norm, conv_w, w_branch_a, w_branch_b, w_branch_c, w_out, w_ffn_up, w_ffn_down):
    B, S, D = x_prompt.shape
    NS = x_sample.shape[0]
    depth = w_in.shape[0]
    Tp = B * S
    tp, ts = _tiles(Tp), _tiles(NS)

    w_in_b = w_in.astype(BF16)
    wa_b, wb_b, wc_b = (w.astype(BF16) for w in (w_branch_a, w_branch_b, w_branch_c))
    wo_b, wu_b, wd_b = (w.astype(BF16) for w in (w_out, w_ffn_up, w_ffn_down))
    fnorm = final_norm[None]
    ck_t = jnp.transpose(cache_k, (0, 1, 3, 4, 2))
    cv_t = jnp.transpose(cache_v, (0, 1, 3, 4, 2))

    xp = x_prompt.reshape(Tp, D)
    xs = x_sample.reshape(NS, D)
    sp_l, cp_l, ks_l, vs_l, ss_l, cs_l = ([] for _ in range(6))
    kt_all = vt_all = None
    for l in range(depth):
        last = l == depth - 1
        n1, n2, bg = norm1[l][None], norm2[l][None], b_gate[l][None]
        gain = hg_norm[l][None]

        p = _inproj(xp, n1, w_in_b, l, tp["inproj_tm"], tp["inproj_tn"])
        p3 = p.reshape(B, S, -1)
        oa, st = _hgrn_prompt(p3, hg_lb_raw, gain, l, tp["mixer_tc"])
        ob, cbuf = _conv_prompt(p3, conv_w[l], tp["mixer_tc"])
        oc, kt_all, vt_all = _moba_prompt(p3, l, depth, kt_all, vt_all)
        x1 = _merge(oa.reshape(Tp, -1), ob.reshape(Tp, -1), oc.reshape(Tp, -1), p, bg, xp,
                    wa_b, wb_b, wc_b, wo_b, l, tp["merge_tm"])
        xp = _ffn(x1, n2, wu_b, wd_b, l, fnorm, last, tp["ffn_tm"], tp["ffn_tf"])
        sp_l.append(st)
        cp_l.append(cbuf)

        ps = _inproj(xs, n1, w_in_b, l, ts["inproj_tm"], ts["inproj_tn"])
        ps3 = ps.reshape(NS, 1, -1)
        oa_s, st_s = _hgrn_sample(ps3, state_hgrn, hg_lb_raw, gain, l)
        ob_s, z_s = _conv_sample(ps, conv_w[l], state_conv[:, l, 0], state_conv[:, l, 1])
        q_s = ps[:, OFF_QC:OFF_QC + ATT_WIDTH]
        k_s = ps[:, OFF_KC:OFF_KC + ATT_WIDTH]
        v_s = ps[:, OFF_VC:OFF_VC + ATT_WIDTH]
        sel = _moba_select(ck_t, page_table, q_s.reshape(NS, 1, ATT_WIDTH), l)
        idx = sel[:, :, :MOBA_TOPK].reshape(-1)
        to4 = lambda a: a.reshape(NS, ATT_HEADS, 1, ATT_HD)
        oc_s = _moba_sample_attend(ck_t, cv_t, page_table, idx, to4(q_s), to4(k_s), to4(v_s), l)
        x1s = _merge(oa_s.reshape(NS, -1), ob_s, oc_s.reshape(NS, -1), ps, bg, xs,
                     wa_b, wb_b, wc_b, wo_b, l, ts["merge_tm"])
        xs = _ffn(x1s, n2, wu_b, wd_b, l, fnorm, last, ts["ffn_tm"], ts["ffn_tf"])
        ks_l.append(k_s.reshape(NS, 1, ATT_HEADS, ATT_HD))
        vs_l.append(v_s.reshape(NS, 1, ATT_HEADS, ATT_HD))
        ss_l.append(st_s)
        cs_l.append(jnp.stack([state_conv[:, l, 1], z_s], axis=1))

    from_t = lambda a: jnp.transpose(a.reshape(B, depth, ATT_HEADS, ATT_HD, S), (0, 1, 4, 2, 3))
    return (xp.reshape(B, S, D), xs.reshape(NS, 1, D),
            from_t(kt_all), from_t(vt_all),
            jnp.stack(sp_l, axis=1), jnp.stack(cp_l, axis=1),
            jnp.stack(ks_l, axis=1), jnp.stack(vs_l, axis=1),
            jnp.stack(ss_l, axis=1), jnp.stack(cs_l, axis=1))
```

```python
import functools

import jax
import jax.numpy as jnp
from jax import lax
from jax.experimental import pallas as pl
from jax.experimental.pallas import tpu as pltpu

F32 = jnp.float32
BF16 = jnp.bfloat16

EPS = 1e-6
MASK_VALUE = -1e30
EXP_CLAMP = 60.0
HG_HEADS = 4
HG_DK = 128
HG_DV = 128
HG_WIDTH = HG_HEADS * HG_DK
HG_CHUNK = 32
CONV_DIM = 512
CONV_W = 3
ATT_HEADS = 8
ATT_HD = 64
ATT_WIDTH = ATT_HEADS * ATT_HD
MOBA_BLOCK = 256
MOBA_TOPK = 3
N_BRANCH = 3
LANES = 128
HEADS_PER_LANE_TILE = LANES // ATT_HD

OFF_QA, OFF_FA, OFF_IA, OFF_OGA = 0, 512, 1024, 1536
OFF_BG, OFF_CG, OFF_HC = 2048, 2560, 3072
OFF_QC, OFF_KC, OFF_VC = 3584, 4096, 4608
OFF_GATE = 5120

VMEM_LIMIT = 56 * 1024 * 1024
HIGHEST = lax.Precision.HIGHEST


def _params(n_axes):
    return pltpu.CompilerParams(dimension_semantics=("arbitrary",) * n_axes,
                                vmem_limit_bytes=VMEM_LIMIT)


def _dot(a, b):
    return jnp.dot(a, b, preferred_element_type=F32)


def _dot_nt(a, b, precision=None):
    return lax.dot_general(a, b, (((1,), (1,)), ((), ())),
                           preferred_element_type=F32, precision=precision)


def _dot_tn(a, b):
    return lax.dot_general(a, b, (((0,), (0,)), ((), ())), preferred_element_type=F32)


def _rms(x):
    return x * lax.rsqrt(jnp.mean(x * x, axis=-1, keepdims=True) + EPS)


def _silu(x):
    h = 0.5 * x
    return h + h * jnp.tanh(h)


def _log_sigmoid(a):
    return jnp.minimum(a, 0.0) - jnp.log1p(jnp.exp(-jnp.abs(a)))


def _hgrn_gate_inputs(qa, fa, lb):
    q = _silu(qa)
    logf = _log_sigmoid(fa) + jnp.log1p(lb * jnp.exp(jnp.minimum(-fa, EXP_CLAMP)))
    t = jnp.tanh(0.5 * logf)
    k = -2.0 * t / (1.0 - t)
    return q, k, logf


def _layer_lower_bound(raw, layer):
    e = jnp.exp(raw - jnp.max(raw, axis=0, keepdims=True))
    pr = e / jnp.sum(e, axis=0, keepdims=True)
    lb = jnp.zeros_like(pr[0:1])
    for i in range(1, layer + 1):
        lb = lb + pr[i:i + 1]
    return lb


def _inproj_body(x_ref, g_ref, w_ref, o_ref, h_scr):
    @pl.when(pl.program_id(1) == 0)
    def _():
        h_scr[...] = (_rms(x_ref[...]) * g_ref[...]).astype(BF16)

    o_ref[...] = _dot(h_scr[...], w_ref[...])


def _inproj(x2d, gain, w_bf, layer, tm, tn):
    T, D = x2d.shape
    N = w_bf.shape[2]
    return pl.pallas_call(
        _inproj_body,
        grid=(T // tm, N // tn),
        in_specs=[pl.BlockSpec((tm, D), lambda i, j: (i, 0)),
                  pl.BlockSpec((1, D), lambda i, j: (0, 0)),
                  pl.BlockSpec((None, D, tn), lambda i, j: (layer, 0, j))],
        out_specs=pl.BlockSpec((tm, tn), lambda i, j: (i, j)),
        out_shape=jax.ShapeDtypeStruct((T, N), F32),
        scratch_shapes=[pltpu.VMEM((tm, D), BF16)],
        compiler_params=_params(2),
        name="inproj",
    )(x2d, gain, w_bf)


def _chunk_cumsum(x, chunk):
    rin = lax.broadcasted_iota(jnp.int32, x.shape, 0) % chunk
    shift = 1
    while shift < chunk:
        x = x + jnp.where(rin >= shift, pltpu.roll(x, shift, 0), 0.0)
        shift *= 2
    return x


def _hgrn_prompt_body(layer, nchunk, q_ref, f_ref, i_ref, og_ref, lbr_ref, gn_ref,
                      oa_ref, st_ref, st_s):
    C = HG_CHUNK
    t = pl.program_id(2)
    tc = q_ref.shape[0]

    @pl.when(t == 0)
    def _():
        st_s[...] = jnp.zeros_like(st_s)

    lb = _layer_lower_bound(lbr_ref[...], layer)
    q, k, logf = _hgrn_gate_inputs(q_ref[...], f_ref[...], lb)
    b = _chunk_cumsum(logf, C)

    def per_chunk_row(r):
        rows = b.reshape(nchunk, C, HG_DK)[:, r:r + 1, :]
        return jnp.broadcast_to(rows, (nchunk, C, HG_DK)).reshape(tc, HG_DK)

    b_ref = per_chunk_row(C // 2)
    b_last = per_chunk_row(C - 1)
    qx = (q * jnp.exp(b - b_ref)).astype(BF16)
    kx = (k * jnp.exp(b_ref - b)).astype(BF16)
    qd = (q * jnp.exp(b)).astype(BF16)
    kd = (k * jnp.exp(b_last - b)).astype(BF16)
    decay = jnp.exp(b_last)
    vb = i_ref[...].astype(BF16)

    row = lax.broadcasted_iota(jnp.int32, (C, C), 0)
    col = lax.broadcasted_iota(jnp.int32, (C, C), 1)
    causal = row >= col

    st = st_s[...]
    outs = []
    for c in range(nchunk):
        rows = slice(c * C, (c + 1) * C)
        a = jnp.where(causal, _dot_nt(qx[rows], kx[rows]), 0.0)
        o = _dot(a.astype(BF16), vb[rows]) + _dot_nt(qd[rows], st.astype(BF16))
        st = st * decay[c * C:c * C + 1] + _dot_tn(vb[rows], kd[rows])
        outs.append(o)
    st_s[...] = st

    o = _rms(jnp.concatenate(outs, axis=0)) * gn_ref[...]
    oa_ref[...] = (o * _silu(og_ref[...])).astype(oa_ref.dtype)

    @pl.when(t == pl.num_programs(2) - 1)
    def _():
        st_ref[...] = st.T


def _hgrn_prompt(p3, lb_raw, gain, layer, tc):
    B, S, _ = p3.shape
    depth = lb_raw.shape[0]
    blk = lambda off: pl.BlockSpec((None, tc, HG_DK),
                                   lambda b, h, t, o=off // HG_DK: (b, t, o + h))
    return pl.pallas_call(
        functools.partial(_hgrn_prompt_body, layer, tc // HG_CHUNK),
        grid=(B, HG_HEADS, S // tc),
        in_specs=[blk(OFF_QA), blk(OFF_FA), blk(OFF_IA), blk(OFF_OGA),
                  pl.BlockSpec((depth, HG_DK), lambda b, h, t: (0, h)),
                  pl.BlockSpec((1, HG_DV), lambda b, h, t: (0, 0))],
        out_specs=[pl.BlockSpec((None, tc, HG_DV), lambda b, h, t: (b, t, h)),
                   pl.BlockSpec((None, None, HG_DK, HG_DV), lambda b, h, t: (b, h, 0, 0))],
        out_shape=[jax.ShapeDtypeStruct((B, S, HG_HEADS * HG_DV), BF16),
                   jax.ShapeDtypeStruct((B, HG_HEADS, HG_DK, HG_DV), F32)],
        scratch_shapes=[pltpu.VMEM((HG_DV, HG_DK), F32)],
        compiler_params=_params(3),
        name="hgrn_prompt",
    )(p3, p3, p3, p3, lb_raw, gain)


def _as_column(row):
    n = row.shape[-1]
    return jnp.broadcast_to(row, (n, n)).T


def _hgrn_sample_body(layer, q_ref, f_ref, i_ref, og_ref, lbr_ref, gn_ref, s0_ref,
                      oa_ref, st_ref):
    lb_all = _layer_lower_bound(lbr_ref[...], layer)
    q_all, k_all, logf_all = _hgrn_gate_inputs(q_ref[...], f_ref[...], lb_all)
    v_all, og_all = i_ref[...], og_ref[...]
    for h in range(HG_HEADS):
        hs = slice(h * HG_DK, (h + 1) * HG_DK)
        s = _as_column(jnp.exp(logf_all[:, hs])) * s0_ref[h] + _as_column(k_all[:, hs]) * v_all[:, hs]
        st_ref[h] = s
        o = jnp.sum(_as_column(q_all[:, hs]) * s, axis=0, keepdims=True)
        o = _rms(o) * gn_ref[...]
        oa_ref[:, hs] = (o * _silu(og_all[:, hs])).astype(oa_ref.dtype)


def _hgrn_sample(ps3, state, lb_raw, gain, layer):
    NS = ps3.shape[0]
    depth = lb_raw.shape[0]
    blk = lambda off: pl.BlockSpec((None, 1, HG_WIDTH), lambda b, o=off // HG_WIDTH: (b, 0, o))
    return pl.pallas_call(
        functools.partial(_hgrn_sample_body, layer),
        grid=(NS,),
        in_specs=[blk(OFF_QA), blk(OFF_FA), blk(OFF_IA), blk(OFF_OGA),
                  pl.BlockSpec((depth, HG_WIDTH), lambda b: (0, 0)),
                  pl.BlockSpec((1, HG_DV), lambda b: (0, 0)),
                  pl.BlockSpec((None, None, HG_HEADS, HG_DK, HG_DV),
                               lambda b: (b, layer, 0, 0, 0))],
        out_specs=[pl.BlockSpec((None, 1, HG_HEADS * HG_DV), lambda b: (b, 0, 0)),
                   pl.BlockSpec((None, HG_HEADS, HG_DK, HG_DV), lambda b: (b, 0, 0, 0))],
        out_shape=[jax.ShapeDtypeStruct((NS, 1, HG_HEADS * HG_DV), BF16),
                   jax.ShapeDtypeStruct((NS, HG_HEADS, HG_DK, HG_DV), F32)],
        compiler_params=_params(1),
        name="hgrn_sample",
    )(ps3, ps3, ps3, ps3, lb_raw, gain, state)


CONV_PAD = 8


def _conv_prompt_body(bg_ref, cg_ref, hc_ref, w_ref, ob_ref, buf_ref, z_s):
    t = pl.program_id(1)
    tc = bg_ref.shape[0]

    @pl.when(t == 0)
    def _():
        z_s[0:CONV_PAD, :] = jnp.zeros((CONV_PAD, CONV_DIM), F32)

    z = cg_ref[...] * hc_ref[...]
    z_s[CONV_PAD:CONV_PAD + tc, :] = z
    w = w_ref[...]
    y = w[CONV_W - 1:CONV_W] * z
    for i in range(CONV_W - 1):
        shift = CONV_W - 1 - i
        y = y + w[i:i + 1] * z_s[CONV_PAD - shift:CONV_PAD - shift + tc, :]
    ob_ref[...] = (bg_ref[...] * y).astype(ob_ref.dtype)
    z_s[0:CONV_PAD, :] = z[tc - CONV_PAD:tc, :]

    @pl.when(t == pl.num_programs(1) - 1)
    def _():
        buf_ref[...] = z[tc - (CONV_W - 1):tc, :]


def _conv_prompt(p3, w, tc):
    B, S, _ = p3.shape
    blk = lambda off: pl.BlockSpec((None, tc, CONV_DIM),
                                   lambda b, t, o=off // CONV_DIM: (b, t, o))
    return pl.pallas_call(
        _conv_prompt_body,
        grid=(B, S // tc),
        in_specs=[blk(OFF_BG), blk(OFF_CG), blk(OFF_HC),
                  pl.BlockSpec((CONV_W, CONV_DIM), lambda b, t: (0, 0))],
        out_specs=[pl.BlockSpec((None, tc, CONV_DIM), lambda b, t: (b, t, 0)),
                   pl.BlockSpec((None, CONV_W - 1, CONV_DIM), lambda b, t: (b, 0, 0))],
        out_shape=[jax.ShapeDtypeStruct((B, S, CONV_DIM), BF16),
                   jax.ShapeDtypeStruct((B, CONV_W - 1, CONV_DIM), F32)],
        scratch_shapes=[pltpu.VMEM((tc + CONV_PAD, CONV_DIM), F32)],
        compiler_params=_params(2),
        name="conv_prompt",
    )(p3, p3, p3, w)


def _conv_sample_body(bg_ref, cg_ref, hc_ref, w_ref, b0_ref, b1_ref, ob_ref, z_ref):
    z = cg_ref[...] * hc_ref[...]
    w = w_ref[...]
    y = w[0:1] * b0_ref[...] + w[1:2] * b1_ref[...] + w[2:3] * z
    ob_ref[...] = (bg_ref[...] * y).astype(ob_ref.dtype)
    z_ref[...] = z


def _conv_sample(ps, w, buf0, buf1):
    NS = ps.shape[0]
    blk = lambda off: pl.BlockSpec((NS, CONV_DIM), lambda i, o=off // CONV_DIM: (0, o))
    full = pl.BlockSpec((NS, CONV_DIM), lambda i: (0, 0))
    return pl.pallas_call(
        _conv_sample_body,
        grid=(1,),
        in_specs=[blk(OFF_BG), blk(OFF_CG), blk(OFF_HC),
                  pl.BlockSpec((CONV_W, CONV_DIM), lambda i: (0, 0)), full, full],
        out_specs=[full, full],
        out_shape=[jax.ShapeDtypeStruct((NS, CONV_DIM), BF16),
                   jax.ShapeDtypeStruct((NS, CONV_DIM), F32)],
        compiler_params=_params(1),
        name="conv_sample",
    )(ps, ps, ps, w, buf0, buf1)


def _moba_prompt_body(nb, n_carried, q_ref, k_ref, v_ref, *refs):
    (oc_ref, kt_ref, vt_ref,
     kb_s, vt_s, mean_s, msplit_s, bias_s, qb_s, s_buf, p_buf) = refs[n_carried:]
    L = MOBA_BLOCK
    nh = HEADS_PER_LANE_TILE

    for blk in range(nb):
        rows = slice(blk * L, (blk + 1) * L)
        kblk = k_ref[rows, :]
        kb_s[blk] = kblk.astype(BF16)
        mean_s[blk:blk + 1, :] = jnp.mean(kblk, axis=0, keepdims=True)
        kt_ref[:, rows] = kblk.T
        vt = v_ref[rows, :].T
        vt_ref[:, rows] = vt
        vt_s[blk] = vt.astype(BF16)
    mean = mean_s[...]
    mean_head = lax.broadcasted_iota(jnp.int32, mean.shape, 1) // ATT_HD
    for h in range(nh):
        mh = jnp.where(mean_head == h, mean, 0.0)
        hi = mh.astype(BF16)
        msplit_s[h * nb:(h + 1) * nb, :] = hi
        msplit_s[(nh + h) * nb:(nh + h + 1) * nb, :] = (mh - hi.astype(F32)).astype(BF16)

    npc = L // LANES
    pieces = [(h, c) for h in range(nh) for c in range(npc)]
    blk_id = lax.broadcasted_iota(jnp.int32, (nb, L), 0)
    lane_head = lax.broadcasted_iota(jnp.int32, (L, LANES), 1) // ATT_HD
    causal = (lax.broadcasted_iota(jnp.int32, (L, L), 0)
              <= lax.broadcasted_iota(jnp.int32, (L, L), 1))

    def q_rows(n):
        return pl.ds(pl.multiple_of(n * L, L), L)

    def pv_update(blk, slot, alpha, acc):
        vt = vt_s[blk]
        out = []
        for h in range(nh):
            p = jnp.concatenate([p_buf[slot, h, c] for c in range(npc)], axis=1)
            pv = _dot(vt[h * ATT_HD:(h + 1) * ATT_HD], p)
            out += [alpha[h * npc + c] * acc[h * npc + c] + pv[:, c * LANES:(c + 1) * LANES]
                    for c in range(npc)]
        return out

    def softmax_update(slot, bias_rows, m, l):
        m_out, l_out, alpha = [], [], []
        for i, (h, c) in enumerate(pieces):
            cols = slice(c * LANES, (c + 1) * LANES)
            s = s_buf[slot, h, c]
            b = bias_rows[h][:, cols]
            m_new = jnp.maximum(m[i], jnp.max(s, axis=0, keepdims=True) + b)
            a = jnp.exp(m[i] - m_new)
            p = jnp.exp(s - (m_new - b))
            p_buf[slot, h, c] = p.astype(BF16)
            m_out.append(m_new)
            l_out.append(a * l[i] + jnp.sum(p, axis=0, keepdims=True))
            alpha.append(a)
        return m_out, l_out, alpha

    def stage_scores(slot, h, s):
        for c in range(npc):
            s_buf[slot, h, c] = s[:, c * LANES:(c + 1) * LANES]

    def scores_into(slot, blk):
        kblk = kb_s[blk]
        for h in range(nh):
            stage_scores(slot, h, _dot_nt(kblk, qb_s[h]))

    def prepare(n):
        qf = q_ref[q_rows(n), :]
        q_hi = qf.astype(BF16)
        q_lo = (qf - q_hi.astype(F32)).astype(BF16)
        g_hi = _dot_nt(msplit_s[...], q_hi)
        gates = g_hi[:nh * nb] + g_hi[nh * nb:] + _dot_nt(msplit_s[:nh * nb, :], q_lo)
        for h in range(nh):
            qh = jnp.where(lane_head == h, qf, 0.0)
            qb_s[h] = (qh * (ATT_HD ** -0.5)).astype(BF16)
            gate = jnp.where(blk_id < n, gates[h * nb:(h + 1) * nb], MASK_VALUE)
            cnt = jnp.zeros((nb, L), jnp.int32)
            for jp in range(nb):
                gj = gate[jp:jp + 1, :]
                beats = jnp.where(gj > gate, 1, jnp.where((gj == gate) & (blk_id > jp), 1, 0))
                cnt = cnt + beats
            bias_s[h] = jnp.where((cnt < MOBA_TOPK) & (blk_id < n), 0.0, MASK_VALUE)
        k_own = kb_s[n]
        for h in range(nh):
            stage_scores(1, h, jnp.where(causal, _dot_nt(k_own, qb_s[h]), MASK_VALUE))
        scores_into(0, 0)
        m, l, alpha = softmax_update(1, [jnp.zeros((1, L), F32)] * nh,
                                     [jnp.full((1, LANES), -jnp.inf, F32)] * len(pieces),
                                     [jnp.zeros((1, LANES), F32)] * len(pieces))
        return tuple(m), tuple(l), tuple(alpha)

    def stage(prev_blk, blk, next_blk, cur, oth, m, l, alpha, acc):
        acc = pv_update(prev_blk, oth, alpha, acc)
        if next_blk is not None:
            scores_into(oth, next_blk)
        m, l, alpha = softmax_update(cur, [bias_s[h, pl.ds(blk, 1), :] for h in range(nh)], m, l)
        return m, l, alpha, acc

    def query_block(n, state):
        m, l, alpha = (list(c) for c in state)
        acc = [jnp.zeros((ATT_HD, LANES), F32)] * len(pieces)

        def double_step(t, carry):
            m, l, alpha, acc = (list(c) for c in carry)
            j = 2 * t
            m, l, alpha, acc = stage(jnp.where(t == 0, n, j - 1), j, j + 1, 0, 1, m, l, alpha, acc)
            m, l, alpha, acc = stage(j, j + 1, jnp.minimum(j + 2, nb - 1), 1, 0, m, l, alpha, acc)
            return tuple(m), tuple(l), tuple(alpha), tuple(acc)

        def odd_tail(carry):
            m, l, alpha, acc = (list(c) for c in carry)
            m, l, alpha, acc = stage(jnp.where(n == 1, n, n - 2), n - 1, None, 0, 1, m, l, alpha, acc)
            return tuple(m), tuple(l), tuple(alpha), tuple(acc)

        carry = lax.fori_loop(0, n // 2, double_step,
                              (tuple(m), tuple(l), tuple(alpha), tuple(acc)))
        odd = (n & 1) == 1
        m, l, alpha, acc = lax.cond(odd, odd_tail, lambda c: c, carry)
        acc = pv_update(jnp.where(n == 0, n, n - 1), jnp.where(odd, 0, 1), alpha, acc)
        out_t = jnp.concatenate(
            [jnp.concatenate([acc[h * npc + c] / l[h * npc + c] for c in range(npc)], axis=1)
             for h in range(nh)], axis=0)
        oc_ref[q_rows(n), :] = out_t.T.astype(oc_ref.dtype)
        return prepare(jnp.minimum(n + 1, nb - 1))

    lax.fori_loop(0, nb, query_block, prepare(0))


def _moba_prompt(p3, layer, depth, kt_all=None, vt_all=None):
    B, S, _ = p3.shape
    nb = S // MOBA_BLOCK
    npair = ATT_HEADS // HEADS_PER_LANE_TILE
    col = lambda off: pl.BlockSpec((None, S, LANES), lambda b, g, o=off // LANES: (b, 0, o + g))
    kv_out = pl.BlockSpec((None, None, LANES, S), lambda b, g: (b, layer, g, 0))
    carried = [] if kt_all is None else [kt_all, vt_all]
    return pl.pallas_call(
        functools.partial(_moba_prompt_body, nb, len(carried)),
        grid=(B, npair),
        in_specs=[col(OFF_QC), col(OFF_KC), col(OFF_VC)]
        + [pl.BlockSpec(memory_space=pl.ANY)] * len(carried),
        out_specs=[pl.BlockSpec((None, S, LANES), lambda b, g: (b, 0, g)), kv_out, kv_out],
        out_shape=[jax.ShapeDtypeStruct((B, S, ATT_WIDTH), BF16),
                   jax.ShapeDtypeStruct((B, depth, ATT_WIDTH, S), F32),
                   jax.ShapeDtypeStruct((B, depth, ATT_WIDTH, S), F32)],
        input_output_aliases={3: 1, 4: 2} if carried else {},
        scratch_shapes=[pltpu.VMEM((nb, MOBA_BLOCK, LANES), BF16),
                        pltpu.VMEM((nb, LANES, MOBA_BLOCK), BF16),
                        pltpu.VMEM((nb, LANES), F32),
                        pltpu.VMEM((2 * HEADS_PER_LANE_TILE * nb, LANES), BF16),
                        pltpu.VMEM((HEADS_PER_LANE_TILE, nb, MOBA_BLOCK), F32),
                        pltpu.VMEM((HEADS_PER_LANE_TILE, MOBA_BLOCK, LANES), BF16),
                        pltpu.VMEM((2, HEADS_PER_LANE_TILE, MOBA_BLOCK // LANES, MOBA_BLOCK, LANES), F32),
                        pltpu.VMEM((2, HEADS_PER_LANE_TILE, MOBA_BLOCK // LANES, MOBA_BLOCK, LANES), BF16)],
        compiler_params=_params(2),
        name="moba_prompt",
    )(p3, p3, p3, *carried)


PAGES_PER_STEP = 32


def _moba_select_body(n_blocks, pages_per_block, pt_ref, q_ref, *refs):
    pages = refs[:PAGES_PER_STEP]
    idx_ref = refs[PAGES_PER_STEP]
    qcol_s, gate_s = refs[PAGES_PER_STEP + 1:]
    g = pl.program_id(1)
    blocks_per_step = PAGES_PER_STEP // pages_per_block

    @pl.when(g == 0)
    def _():
        q = q_ref[...]
        for c in range(ATT_WIDTH // LANES):
            qcol_s[c * LANES:(c + 1) * LANES, :] = _as_column(q[:, c * LANES:(c + 1) * LANES])
        gate_s[...] = jnp.zeros_like(gate_s)

    q_col = qcol_s[...].reshape(ATT_HEADS, ATT_HD, LANES)
    lane = lax.broadcasted_iota(jnp.int32, gate_s.shape, 1)
    acc = gate_s[...]
    for blk in range(blocks_per_step):
        rows = jnp.sum(pages[blk * pages_per_block][...] * q_col, axis=1)
        for i in range(1, pages_per_block):
            rows = rows + jnp.sum(pages[blk * pages_per_block + i][...] * q_col, axis=1)
        col = jnp.sum(rows, axis=1, keepdims=True)
        acc = jnp.where(lane == g * blocks_per_step + blk, col, acc)
    gate_s[...] = acc

    @pl.when(g == pl.num_programs(1) - 1)
    def _():
        gate = acc * (1.0 / MOBA_BLOCK)
        ln = lax.broadcasted_iota(jnp.int32, gate.shape, 1)
        lnf = ln.astype(F32)
        gate = jnp.where(ln < n_blocks, gate, MASK_VALUE)
        out = jnp.zeros(gate.shape, F32)
        for r in range(MOBA_TOPK):
            top = jnp.max(gate, axis=1, keepdims=True)
            ix = jnp.min(jnp.where(gate == top, lnf, float(LANES)), axis=1, keepdims=True)
            out = jnp.where(ln == r, ix, out)
            gate = jnp.where(lnf == ix, -jnp.inf, gate)
        idx_ref[...] = out.astype(jnp.int32)


def _moba_select(ck_t, page_table, q3, layer):
    NS, n_pages = page_table.shape
    page = ck_t.shape[-1]
    pages_per_block = MOBA_BLOCK // page
    n_blocks = n_pages // pages_per_block
    assert n_blocks <= LANES and n_pages % PAGES_PER_STEP == 0 and page == LANES
    page_spec = lambda i: pl.BlockSpec(
        (None, None, ATT_HEADS, ATT_HD, page),
        lambda b, g, pt, i=i: (pt[b, g * PAGES_PER_STEP + i], layer, 0, 0, 0))
    grid_spec = pltpu.PrefetchScalarGridSpec(
        num_scalar_prefetch=1,
        grid=(NS, n_pages // PAGES_PER_STEP),
        in_specs=[pl.BlockSpec((None, 1, ATT_WIDTH), lambda b, g, pt: (b, 0, 0))]
        + [page_spec(i) for i in range(PAGES_PER_STEP)],
        out_specs=pl.BlockSpec((None, ATT_HEADS, LANES), lambda b, g, pt: (b, 0, 0)),
        scratch_shapes=[pltpu.VMEM((ATT_WIDTH, LANES), F32), pltpu.VMEM((ATT_HEADS, LANES), F32)])
    return pl.pallas_call(
        functools.partial(_moba_select_body, n_blocks, pages_per_block),
        grid_spec=grid_spec,
        out_shape=jax.ShapeDtypeStruct((NS, ATT_HEADS, LANES), jnp.int32),
        compiler_params=_params(2),
        name="moba_select",
    )(page_table, q3, *([ck_t] * PAGES_PER_STEP))


ATTEND_HEADS_PER_STEP = 4


def _moba_sample_attend_body(n_tiles, pt_ref, ix_ref, q_ref, kn_ref, vn_ref, *refs):
    hps = ATTEND_HEADS_PER_STEP
    o_ref = refs[2 * hps * n_tiles]
    scale = ATT_HD ** -0.5
    for hh in range(hps):
        k_tiles = refs[hh * n_tiles:(hh + 1) * n_tiles]
        v_tiles = refs[(hps + hh) * n_tiles:(hps + hh + 1) * n_tiles]
        q = q_ref[hh]
        qb = jnp.broadcast_to(q * scale, (8, ATT_HD)).astype(BF16)
        scores = [_dot(qb, kt[...].astype(BF16))[0:1] for kt in k_tiles]
        s_own = jnp.sum(q * kn_ref[hh], axis=1, keepdims=True) * scale
        m = s_own
        for s in scores:
            m = jnp.maximum(m, jnp.max(s, axis=1, keepdims=True))
        p_own = jnp.exp(s_own - m)
        l = p_own
        o = p_own * vn_ref[hh]
        for s, vt in zip(scores, v_tiles):
            p = jnp.exp(s - m)
            l = l + jnp.sum(p, axis=1, keepdims=True)
            pb = jnp.broadcast_to(p, (8, p.shape[1])).astype(BF16)
            o = o + _dot_nt(pb, vt[...].astype(BF16))[0:1]
        o_ref[hh] = (o / l).astype(o_ref.dtype)


def _moba_sample_attend(ck_t, cv_t, page_table, idx_flat, q4, kn4, vn4, layer):
    NS = page_table.shape[0]
    page = ck_t.shape[-1]
    pages_per_block = MOBA_BLOCK // page
    n_tiles = MOBA_TOPK * pages_per_block
    hps = ATTEND_HEADS_PER_STEP

    def tile_spec(hh, i):
        r, pg = divmod(i, pages_per_block)

        def index(b, hb, pt, ix):
            h = hb * hps + hh
            blk = ix[(b * ATT_HEADS + h) * MOBA_TOPK + r]
            return pt[b, blk * pages_per_block + pg], layer, h, 0, 0

        return pl.BlockSpec((None, None, None, ATT_HD, page), index)

    tiles = [tile_spec(hh, i) for hh in range(hps) for i in range(n_tiles)]
    vec = pl.BlockSpec((None, hps, 1, ATT_HD), lambda b, hb, pt, ix: (b, hb, 0, 0))
    grid_spec = pltpu.PrefetchScalarGridSpec(
        num_scalar_prefetch=2,
        grid=(NS, ATT_HEADS // hps),
        in_specs=[vec, vec, vec] + tiles * 2,
        out_specs=vec)
    return pl.pallas_call(
        functools.partial(_moba_sample_attend_body, n_tiles),
        grid_spec=grid_spec,
        out_shape=jax.ShapeDtypeStruct((NS, ATT_HEADS, 1, ATT_HD), BF16),
        compiler_params=_params(2),
        name="moba_sample_attend",
    )(page_table, idx_flat, q4, kn4, vn4, *([ck_t] * len(tiles)), *([cv_t] * len(tiles)))


def _merge_body(oa_ref, ob_ref, oc_ref, ga_ref, gb_ref, gc_ref, ba_ref, bb_ref, bc_ref,
                x_ref, wa_ref, wb_ref, wc_ref, wo_ref, out_ref):
    mix = (jax.nn.sigmoid(ga_ref[...] + ba_ref[...]) * _dot(oa_ref[...], wa_ref[...])
           + jax.nn.sigmoid(gb_ref[...] + bb_ref[...]) * _dot(ob_ref[...], wb_ref[...])
           + jax.nn.sigmoid(gc_ref[...] + bc_ref[...]) * _dot(oc_ref[...], wc_ref[...]))
    out_ref[...] = x_ref[...] + _dot(mix.astype(BF16), wo_ref[...])


def _merge(oa, ob, oc, p, b_gate, x, wa, wb, wc, wo, layer, tm):
    T, D = x.shape
    W = oa.shape[1]
    br = pl.BlockSpec((tm, W), lambda i: (i, 0))
    gate = lambda k: pl.BlockSpec((tm, D), lambda i, k=k: (i, OFF_GATE // D + k))
    bias = lambda k: pl.BlockSpec((1, D), lambda i, k=k: (0, k))
    wbr = pl.BlockSpec((None, W, D), lambda i: (layer, 0, 0))
    return pl.pallas_call(
        _merge_body,
        grid=(T // tm,),
        in_specs=[br, br, br, gate(0), gate(1), gate(2), bias(0), bias(1), bias(2),
                  pl.BlockSpec((tm, D), lambda i: (i, 0)), wbr, wbr, wbr,
                  pl.BlockSpec((None, D, D), lambda i: (layer, 0, 0))],
        out_specs=pl.BlockSpec((tm, D), lambda i: (i, 0)),
        out_shape=jax.ShapeDtypeStruct((T, D), F32),
        compiler_params=_params(1),
        name="merge",
    )(oa, ob, oc, p, p, p, b_gate, b_gate, b_gate, x, wa, wb, wc, wo)


def _ffn_body(final, tf, x_ref, n2_ref, wg_ref, wu_ref, wd_ref, fn_ref, out_ref):
    x = x_ref[...]
    h = (_rms(x) * n2_ref[...]).astype(BF16)
    acc = None
    for c in range(wd_ref.shape[0] // tf):
        cols = slice(c * tf, (c + 1) * tf)
        act = _silu(_dot(h, wg_ref[:, cols])) * _dot(h, wu_ref[:, cols])
        part = _dot(act.astype(BF16), wd_ref[cols, :])
        acc = part if acc is None else acc + part
    y = x + acc
    if final:
        y = _rms(y) * fn_ref[...]
    out_ref[...] = y


def _ffn(x, n2, w_up, w_down, layer, final_norm, final, tm, tf):
    T, D = x.shape
    d_ff = w_down.shape[1]
    resident = lambda shape, j: pl.BlockSpec((None,) + shape, lambda i: (layer, 0, j),
                                             pipeline_mode=pl.Buffered(1))
    return pl.pallas_call(
        functools.partial(_ffn_body, final, tf),
        grid=(T // tm,),
        in_specs=[pl.BlockSpec((tm, D), lambda i: (i, 0)),
                  pl.BlockSpec((1, D), lambda i: (0, 0)),
                  resident((D, d_ff), 0), resident((D, d_ff), 1), resident((d_ff, D), 0),
                  pl.BlockSpec((1, D), lambda i: (0, 0))],
        out_specs=pl.BlockSpec((tm, D), lambda i: (i, 0)),
        out_shape=jax.ShapeDtypeStruct((T, D), F32),
        compiler_params=_params(1),
        name="ffn",
    )(x, n2, w_up, w_up, w_down, final_norm)


def _tiles(T):
    big = T >= 1024
    return dict(inproj_tm=1024 if big else T, inproj_tn=2048 if big else 1024,
                merge_tm=512 if big else T,
                ffn_tm=512 if big else T, ffn_tf=1408,
                mixer_tc=2048)


def kernel(x_prompt, x_sample, cache_k, cache_v, state_hgrn, state_conv, page_table, norm1, norm2, final_norm, w_in, b_gate, hg_lb_raw, hg_norm, conv_w, w_branch_a, w_branch_b, w_branch_c, w_out, w_ffn_up, w_ffn_down):
    B, S, D = x_prompt.shape
    NS = x_sample.shape[0]
    depth = w_in.shape[0]
    Tp = B * S
    tp, ts = _tiles(Tp), _tiles(NS)

    w_in_b = w_in.astype(BF16)
    wa_b, wb_b, wc_b = (w.astype(BF16) for w in (w_branch_a, w_branch_b, w_branch_c))
    wo_b, wu_b, wd_b = (w.astype(BF16) for w in (w_out, w_ffn_up, w_ffn_down))
    fnorm = final_norm[None]
    ck_t = jnp.transpose(cache_k, (0, 1, 3, 4, 2))
    cv_t = jnp.transpose(cache_v, (0, 1, 3, 4, 2))

    xp = x_prompt.reshape(Tp, D)
    xs = x_sample.reshape(NS, D)
    sp_l, cp_l, ks_l, vs_l, ss_l, cs_l = ([] for _ in range(6))
    kt_all = vt_all = None
    for l in range(depth):
        last = l == depth - 1
        n1, n2, bg = norm1[l][None], norm2[l][None], b_gate[l][None]
        gain = hg_norm[l][None]

        p = _inproj(xp, n1, w_in_b, l, tp["inproj_tm"], tp["inproj_tn"])
        p3 = p.reshape(B, S, -1)
        oa, st = _hgrn_prompt(p3, hg_lb_raw, gain, l, tp["mixer_tc"])
        ob, cbuf = _conv_prompt(p3, conv_w[l], tp["mixer_tc"])
        oc, kt_all, vt_all = _moba_prompt(p3, l, depth, kt_all, vt_all)
        x1 = _merge(oa.reshape(Tp, -1), ob.reshape(Tp, -1), oc.reshape(Tp, -1), p, bg, xp,
                    wa_b, wb_b, wc_b, wo_b, l, tp["merge_tm"])
        xp = _ffn(x1, n2, wu_b, wd_b, l, fnorm, last, tp["ffn_tm"], tp["ffn_tf"])
        sp_l.append(st)
        cp_l.append(cbuf)

        ps = _inproj(xs, n1, w_in_b, l, ts["inproj_tm"], ts["inproj_tn"])
        ps3 = ps.reshape(NS, 1, -1)
        oa_s, st_s = _hgrn_sample(ps3, state_hgrn, hg_lb_raw, gain, l)
        ob_s, z_s = _conv_sample(ps, conv_w[l], state_conv[:, l, 0], state_conv[:, l, 1])
        q_s = ps[:, OFF_QC:OFF_QC + ATT_WIDTH]
        k_s = ps[:, OFF_KC:OFF_KC + ATT_WIDTH]
        v_s = ps[:, OFF_VC:OFF_VC + ATT_WIDTH]
        sel = _moba_select(ck_t, page_table, q_s.reshape(NS, 1, ATT_WIDTH), l)
        idx = sel[:, :, :MOBA_TOPK].reshape(-1)
        to4 = lambda a: a.reshape(NS, ATT_HEADS, 1, ATT_HD)
        oc_s = _moba_sample_attend(ck_t, cv_t, page_table, idx, to4(q_s), to4(k_s), to4(v_s), l)
        x1s = _merge(oa_s.reshape(NS, -1), ob_s, oc_s.reshape(NS, -1), ps, bg, xs,
                     wa_b, wb_b, wc_b, wo_b, l, ts["merge_tm"])
        xs = _ffn(x1s, n2, wu_b, wd_b, l, fnorm, last, ts["ffn_tm"], ts["ffn_tf"])
        ks_l.append(k_s.reshape(NS, 1, ATT_HEADS, ATT_HD))
        vs_l.append(v_s.reshape(NS, 1, ATT_HEADS, ATT_HD))
        ss_l.append(st_s)
        cs_l.append(jnp.stack([state_conv[:, l, 1], z_s], axis=1))

    from_t = lambda a: jnp.transpose(a.reshape(B, depth, ATT_HEADS, ATT_HD, S), (0, 1, 4, 2, 3))
    return (xp.reshape(B, S, D), xs.reshape(NS, 1, D),
            from_t(kt_all), from_t(vt_all),
            jnp.stack(sp_l, axis=1), jnp.stack(cp_l, axis=1),
            jnp.stack(ks_l, axis=1), jnp.stack(vs_l, axis=1),
            jnp.stack(ss_l, axis=1), jnp.stack(cs_l, axis=1))
```
